```python
import jax, jax.numpy as jnp
from jax import lax
import numpy as np

D_MODEL = 1024
BATCH = 8
SEQ = 4096
DEPTH = 2

N_MIXERS = 2
CHUNK = 64
EPS = 1e-6

ML_HEADS = 4
ML_DQK = 128
ML_DV = 256
ML_QK_W = ML_HEADS * ML_DQK
ML_V_W = ML_HEADS * ML_DV
ML_IN_W = 2 * ML_QK_W + 2 * ML_V_W + 2 * ML_HEADS

GD_HEADS = 8
GD_DK = 128
GD_DV = 128
GD_K_W = GD_HEADS * GD_DK
GD_V_W = GD_HEADS * GD_DV
GD_CONV_C = 2 * GD_K_W + GD_V_W
GD_CONV_K = 4
GD_IN_W = GD_CONV_C + GD_V_W + 2 * GD_HEADS

D_FF = 3584
N_EXPERTS = 8
TOP_K = 2
D_FF_EXPERT = 3584

kernel_name = 'hybrid_mlstm_gdn_moe_trunk'


def rms_norm(x, g):
    xf = x.astype(jnp.float32)
    y = xf * lax.rsqrt(jnp.mean(xf * xf, axis=-1, keepdims=True) + EPS)
    return (y * g.astype(jnp.float32)).astype(x.dtype)


def to_chunks(t):
    b, h, s = t.shape[:3]
    t = t.reshape(b, h, s // CHUNK, CHUNK, *t.shape[3:])
    return jnp.moveaxis(t, 2, 0)


def from_chunks(t):
    t = jnp.moveaxis(t, 0, 2)
    return t.reshape(t.shape[0], t.shape[1], -1, *t.shape[4:])


def mlstm_mixer(h, w_in, b_if, head_gain, w_out):
    bsz, s, _ = h.shape
    f32 = jnp.float32
    p = (h @ w_in).astype(f32)
    q, k, v, o, gates = jnp.split(
        p, [ML_QK_W, 2 * ML_QK_W, 2 * ML_QK_W + ML_V_W, 2 * ML_QK_W + 2 * ML_V_W], axis=-1)
    heads = lambda t, d: t.reshape(bsz, s, ML_HEADS, d).transpose(0, 2, 1, 3)
    q = heads(q, ML_DQK)
    k = heads(k, ML_DQK) * (ML_DQK ** -0.5)
    v = heads(v, ML_DV)
    gates = (gates + b_if.astype(f32)).transpose(0, 2, 1)
    log_i = gates[:, :ML_HEADS]
    log_f = jax.nn.log_sigmoid(gates[:, ML_HEADS:])
    causal = jnp.tril(jnp.ones((CHUNK, CHUNK), dtype=bool))

    def step(carry, inp):
        C, n, m = carry
        qc, kc, vc, ic, fc = inp
        b = jnp.cumsum(fc, axis=-1)
        d = jnp.where(causal, b[..., :, None] - b[..., None, :] + ic[..., None, :], -jnp.inf)
        m_inter = b + m[..., None]
        m_t = jnp.maximum(jnp.max(d, axis=-1), m_inter)
        w = jnp.exp(d - m_t[..., None])
        sc = jnp.einsum('bhtd,bhsd->bhts', qc, kc) * w
        a = jnp.exp(m_inter - m_t)
        num = jnp.einsum('bhts,bhsv->bhtv', sc, vc) + a[..., None] * jnp.einsum('bhtd,bhdv->bhtv', qc, C)
        den = jnp.sum(sc, axis=-1) + a * jnp.einsum('bhtd,bhd->bht', qc, n)
        hc = num / jnp.maximum(jnp.abs(den), jnp.exp(-m_t))[..., None]
        b_last = b[..., -1]
        d_last = b_last[..., None] - b + ic
        m_new = jnp.maximum(b_last + m, jnp.max(d_last, axis=-1))
        ws = jnp.exp(d_last - m_new[..., None])
        decay = jnp.exp(b_last + m - m_new)
        C = decay[..., None, None] * C + jnp.einsum('bhs,bhsd,bhsv->bhdv', ws, kc, vc)
        n = decay[..., None] * n + jnp.einsum('bhs,bhsd->bhd', ws, kc)
        return (C, n, m_new), hc

    init = (jnp.zeros((bsz, ML_HEADS, ML_DQK, ML_DV), f32),
            jnp.zeros((bsz, ML_HEADS, ML_DQK), f32),
            jnp.zeros((bsz, ML_HEADS), f32))
    _, hs = lax.scan(step, init, (to_chunks(q), to_chunks(k), to_chunks(v),
                                  to_chunks(log_i), to_chunks(log_f)))
    hs = from_chunks(hs)
    hs = hs * lax.rsqrt(jnp.mean(hs * hs, axis=-1, keepdims=True) + EPS)
    hs = hs * head_gain.astype(f32).reshape(ML_HEADS, 1, ML_DV)
    hs = hs.transpose(0, 2, 1, 3).reshape(bsz, s, ML_V_W) * jax.nn.sigmoid(o)
    return hs.astype(h.dtype) @ w_out


def l2_normalize(t):
    return t * lax.rsqrt(jnp.sum(t * t, axis=-1, keepdims=True) + EPS)


def gdn_mixer(h, w_in, conv_w, a_log, dt_bias, out_gain, w_out):
    bsz, s, _ = h.shape
    f32 = jnp.float32
    p = h @ w_in
    qkv, z, b_raw, a_raw = jnp.split(
        p, [GD_CONV_C, GD_CONV_C + GD_V_W, GD_CONV_C + GD_V_W + GD_HEADS], axis=-1)
    qkv = lax.conv_general_dilated(
        qkv, conv_w[:, None, :].astype(qkv.dtype), window_strides=(1,),
        padding=[(GD_CONV_K - 1, 0)], dimension_numbers=('NWC', 'WIO', 'NWC'),
        feature_group_count=GD_CONV_C)
    qkv = jax.nn.silu(qkv.astype(f32))
    q, k, v = jnp.split(qkv, [GD_K_W, 2 * GD_K_W], axis=-1)
    heads = lambda t, d: t.reshape(bsz, s, GD_HEADS, d).transpose(0, 2, 1, 3)
    q = l2_normalize(heads(q, GD_DK)) * (GD_DK ** -0.5)
    k = l2_normalize(heads(k, GD_DK))
    v = heads(v, GD_DV)
    beta = jax.nn.sigmoid(b_raw.astype(f32)).transpose(0, 2, 1)
    g = -(jnp.exp(a_log.astype(f32)) *
          jax.nn.softplus(a_raw.astype(f32) + dt_bias.astype(f32))).transpose(0, 2, 1)

    qc, kc, vc = to_chunks(q), to_chunks(k), to_chunks(v)
    betac = to_chunks(beta)[..., None]
    gc = jnp.cumsum(to_chunks(g), axis=-1)
    causal = jnp.tril(jnp.ones((CHUNK, CHUNK), dtype=bool))
    strict = jnp.tril(jnp.ones((CHUNK, CHUNK), dtype=bool), -1)
    diff = gc[..., :, None] - gc[..., None, :]
    gamma = jnp.where(causal, jnp.exp(jnp.where(causal, diff, 0.0)), 0.0)
    kb = kc * betac
    A = jnp.where(strict, jnp.einsum('nbhtd,nbhsd->nbhts', kb, kc) * gamma, 0.0)
    u = lax.linalg.triangular_solve(A, vc * betac, left_side=True, lower=True, unit_diagonal=True)
    w = lax.linalg.triangular_solve(A, kb * jnp.exp(gc)[..., None], left_side=True, lower=True,
                                    unit_diagonal=True)
    attn = jnp.einsum('nbhtd,nbhsd->nbhts', qc, kc) * gamma
    qg = qc * jnp.exp(gc)[..., None]
    g_last = gc[..., -1]
    kdec = kc * jnp.exp(g_last[..., None] - gc)[..., None]

    def step(state, inp):
        qg_, attn_, u_, w_, kdec_, gl_ = inp
        v_new = u_ - jnp.einsum('bhtk,bhkv->bhtv', w_, state)
        o = jnp.einsum('bhtk,bhkv->bhtv', qg_, state) + jnp.einsum('bhts,bhsv->bhtv', attn_, v_new)
        state = jnp.exp(gl_)[..., None, None] * state + jnp.einsum('bhsk,bhsv->bhkv', kdec_, v_new)
        return state, o

    init = jnp.zeros((bsz, GD_HEADS, GD_DK, GD_DV), f32)
    _, os_ = lax.scan(step, init, (qg, attn, u, w, kdec, g_last))
    o = from_chunks(os_)
    o = o * lax.rsqrt(jnp.mean(o * o, axis=-1, keepdims=True) + EPS) * out_gain.astype(f32)
    o = o.transpose(0, 2, 1, 3).reshape(bsz, s, GD_V_W) * jax.nn.silu(z.astype(f32))
    return o.astype(h.dtype) @ w_out


def swiglu(h, w_gate, w_up, w_down):
    return (jax.nn.silu(h @ w_gate) * (h @ w_up)) @ w_down


def moe_ffn(h, router, w_gate, w_up, w_down):
    bsz, s, d = h.shape
    t = h.reshape(-1, d)
    logits = (t @ router).astype(jnp.float32)
    top_v, top_i = lax.top_k(logits, TOP_K)
    top_w = jax.nn.softmax(top_v, axis=-1)
    gates = jnp.sum(jax.nn.one_hot(top_i, N_EXPERTS, dtype=jnp.float32) * top_w[..., None], axis=1)
    y = jnp.zeros(t.shape, jnp.float32)
    for e in range(N_EXPERTS):
        y = y + gates[:, e:e + 1] * swiglu(t, w_gate[e], w_up[e], w_down[e]).astype(jnp.float32)
    return y.astype(h.dtype).reshape(bsz, s, d)


def setup_inputs(seed: int = 0) -> dict:
    key = jax.random.key(seed)
    ks = list(jax.random.split(key, 32))
    nrm = lambda k, shape, fan_in: jax.random.normal(k, shape, jnp.float32) * (fan_in ** -0.5)
    gain = lambda k, n: 1.0 + 0.02 * jax.random.normal(k, (n,), jnp.float32)
    b_if = jnp.concatenate([
        0.1 * jax.random.normal(ks[3], (ML_HEADS,), jnp.float32),
        3.0 + 0.5 * jax.random.normal(ks[4], (ML_HEADS,), jnp.float32)])
    a_log = jnp.log(jax.random.uniform(ks[14], (GD_HEADS,), jnp.float32, 1.0, 16.0))
    dt = jnp.exp(jax.random.uniform(ks[15], (GD_HEADS,), jnp.float32, np.log(1e-3), np.log(1e-1)))
    dt_bias = dt + jnp.log(-jnp.expm1(-dt))
    return {
        'x': jax.random.normal(ks[0], (BATCH, SEQ, D_MODEL), jnp.float32),
        'l0_norm_mix': gain(ks[1], D_MODEL),
        'l0_ml_w_in': nrm(ks[2], (D_MODEL, ML_IN_W), D_MODEL),
        'l0_ml_b_if': b_if,
        'l0_ml_head_gain': gain(ks[5], ML_V_W),
        'l0_ml_w_out': nrm(ks[6], (ML_V_W, D_MODEL), ML_V_W),
        'l0_norm_ffn': gain(ks[7], D_MODEL),
        'l0_ffn_w_gate': nrm(ks[8], (D_MODEL, D_FF), D_MODEL),
        'l0_ffn_w_up': nrm(ks[9], (D_MODEL, D_FF), D_MODEL),
        'l0_ffn_w_down': nrm(ks[10], (D_FF, D_MODEL), D_FF),
        'l1_norm_mix': gain(ks[11], D_MODEL),
        'l1_gdn_w_in': nrm(ks[12], (D_MODEL, GD_IN_W), D_MODEL),
        'l1_gdn_conv': nrm(ks[13], (GD_CONV_K, GD_CONV_C), GD_CONV_K),
        'l1_gdn_a_log': a_log,
        'l1_gdn_dt_bias': dt_bias,
        'l1_gdn_out_gain': gain(ks[16], GD_DV),
        'l1_gdn_w_out': nrm(ks[17], (GD_V_W, D_MODEL), GD_V_W),
        'l1_norm_ffn': gain(ks[18], D_MODEL),
        'l1_moe_router': nrm(ks[19], (D_MODEL, N_EXPERTS), D_MODEL),
        'l1_moe_w_gate': nrm(ks[20], (N_EXPERTS, D_MODEL, D_FF_EXPERT), D_MODEL),
        'l1_moe_w_up': nrm(ks[21], (N_EXPERTS, D_MODEL, D_FF_EXPERT), D_MODEL),
        'l1_moe_w_down': nrm(ks[22], (N_EXPERTS, D_FF_EXPERT, D_MODEL), D_FF_EXPERT),
        'final_norm': gain(ks[23], D_MODEL),
    }


def reference(x, l0_norm_mix, l0_ml_w_in, l0_ml_b_if, l0_ml_head_gain, l0_ml_w_out,
              l0_norm_ffn, l0_ffn_w_gate, l0_ffn_w_up, l0_ffn_w_down,
              l1_norm_mix, l1_gdn_w_in, l1_gdn_conv, l1_gdn_a_log, l1_gdn_dt_bias,
              l1_gdn_out_gain, l1_gdn_w_out, l1_norm_ffn, l1_moe_router,
              l1_moe_w_gate, l1_moe_w_up, l1_moe_w_down, final_norm):
    mixers = [
        lambda h: mlstm_mixer(h, l0_ml_w_in, l0_ml_b_if, l0_ml_head_gain, l0_ml_w_out),
        lambda h: gdn_mixer(h, l1_gdn_w_in, l1_gdn_conv, l1_gdn_a_log, l1_gdn_dt_bias,
                            l1_gdn_out_gain, l1_gdn_w_out),
    ]
    ffns = [
        lambda h: swiglu(h, l0_ffn_w_gate, l0_ffn_w_up, l0_ffn_w_down),
        lambda h: moe_ffn(h, l1_moe_router, l1_moe_w_gate, l1_moe_w_up, l1_moe_w_down),
    ]
    norms_mix = [l0_norm_mix, l1_norm_mix]
    norms_ffn = [l0_norm_ffn, l1_norm_ffn]
    for i in range(DEPTH):
        x = x + mixers[i % N_MIXERS](rms_norm(x, norms_mix[i]))
        x = x + ffns[i % 2](rms_norm(x, norms_ffn[i]))
    return rms_norm(x, final_norm)
```

```python
import functools

import jax
import jax.numpy as jnp
from jax import lax
from jax.experimental import pallas as pl
from jax.experimental.pallas import tpu as pltpu

EPS = 1e-6
CHUNK = 64

ML_HEADS, ML_DQK, ML_DV = 4, 128, 256
GD_HEADS, GD_DK, GD_DV = 8, 128, 128
GD_CONV_K = 4
N_EXPERTS, TOP_K = 8, 2

F32 = jnp.float32
BF16 = jnp.bfloat16

_NT = (((1,), (1,)), ((), ()))
_TN = (((0,), (0,)), ((), ()))


def _dot(a, b):
    return jnp.dot(a, b, preferred_element_type=F32)


def _dot_nt(a, b):
    return lax.dot_general(a, b, _NT, preferred_element_type=F32)


def _dot_tn(a, b):
    return lax.dot_general(a, b, _TN, preferred_element_type=F32)


def _rms(x, g):
    return x * lax.rsqrt(jnp.mean(x * x, axis=-1, keepdims=True) + EPS) * g


def _softplus(x):
    return jnp.maximum(x, 0.0) + jnp.log(1.0 + jnp.exp(-jnp.abs(x)))


def _sigmoid(x):
    return 1.0 / (1.0 + jnp.exp(-x))


def _silu(x):
    return x * _sigmoid(x)


def _mlstm_gate_act(pre, gidx, p1, p2):
    del p2
    z = pre + p1
    return jnp.where(gidx < ML_HEADS, z, -_softplus(-z))


def _gdn_gate_act(pre, gidx, p1, p2):
    beta = _sigmoid(pre)
    g = -(jnp.exp(p2) * _softplus(pre + p1))
    return jnp.where(gidx < GD_HEADS, beta, g)


def _inproj_kernel(x_ref, g_ref, w_ref, wg_ref, wgt_ref, pc_ref, pr_ref,
                   main_ref, gcol_ref, grow_ref, h_ref, *, act):
    j = pl.program_id(1)

    @pl.when(j == 0)
    def _():
        h = _rms(x_ref[...], g_ref[...]).astype(BF16)
        h_ref[...] = h
        ng = wg_ref.shape[1]
        pre_c = _dot(h, wg_ref[...])
        gi_c = lax.broadcasted_iota(jnp.int32, pre_c.shape, 1)
        gcol_ref[...] = act(pre_c, gi_c, pc_ref[0:1, :], pc_ref[1:2, :])
        pre_r = _dot_nt(wgt_ref[...], h)
        gi_r = lax.broadcasted_iota(jnp.int32, pre_r.shape, 0)
        grow_ref[...] = act(pre_r, gi_r, pr_ref[:, 0:1], pr_ref[:, 1:2])
        del ng

    main_ref[...] = _dot(h_ref[...], w_ref[...]).astype(main_ref.dtype)


def _inproj(x2, gain, w_main, w_gate, p1, p2, act, *, tm, tn):
    n, d = x2.shape
    fm = w_main.shape[1]
    ng = w_gate.shape[1]
    pc = jnp.stack([p1, p2]).astype(F32)
    pr = pc.T
    return pl.pallas_call(
        functools.partial(_inproj_kernel, act=act),
        grid=(n // tm, fm // tn),
        in_specs=[
            pl.BlockSpec((tm, d), lambda i, j: (i, 0)),
            pl.BlockSpec((1, d), lambda i, j: (0, 0)),
            pl.BlockSpec((d, tn), lambda i, j: (0, j)),
            pl.BlockSpec((d, ng), lambda i, j: (0, 0)),
            pl.BlockSpec((ng, d), lambda i, j: (0, 0)),
            pl.BlockSpec((2, ng), lambda i, j: (0, 0)),
            pl.BlockSpec((ng, 2), lambda i, j: (0, 0)),
        ],
        out_specs=[
            pl.BlockSpec((tm, tn), lambda i, j: (i, j)),
            pl.BlockSpec((tm, ng), lambda i, j: (i, 0)),
            pl.BlockSpec((ng, tm), lambda i, j: (0, i)),
        ],
        out_shape=[
            jax.ShapeDtypeStruct((n, fm), BF16),
            jax.ShapeDtypeStruct((n, ng), F32),
            jax.ShapeDtypeStruct((ng, n), F32),
        ],
        scratch_shapes=[pltpu.VMEM((tm, d), BF16)],
        compiler_params=pltpu.CompilerParams(dimension_semantics=("parallel", "arbitrary")),
        name="inproj",
    )(x2, gain.reshape(1, d), w_main.astype(BF16), w_gate.astype(BF16), w_gate.T.astype(BF16), pc, pr)


def _outproj_kernel(a_ref, w_ref, res_ref, out_ref):
    out_ref[...] = res_ref[...] + _dot(a_ref[...], w_ref[...])


def _outproj(a, w, res, *, tm):
    n, k = a.shape
    d = w.shape[1]
    return pl.pallas_call(
        _outproj_kernel,
        grid=(n // tm,),
        in_specs=[
            pl.BlockSpec((tm, k), lambda i: (i, 0)),
            pl.BlockSpec((k, d), lambda i: (0, 0)),
            pl.BlockSpec((tm, d), lambda i: (i, 0)),
        ],
        out_specs=pl.BlockSpec((tm, d), lambda i: (i, 0)),
        out_shape=jax.ShapeDtypeStruct((n, d), F32),
        compiler_params=pltpu.CompilerParams(dimension_semantics=("parallel",)),
        name="outproj",
    )(a, w.astype(BF16), res)


def _cumsum_col_row(v_col, v_row, incl_lower, incl_upper):
    c_col = jnp.sum(jnp.where(incl_lower, v_row, 0.0), axis=1, keepdims=True)
    c_row = jnp.sum(jnp.where(incl_upper, v_col, 0.0), axis=0, keepdims=True)
    return c_col, c_row


def _mlstm_kernel(q_ref, k_ref, v_ref, o_ref, gcol_ref, grow_ref, gain_ref, out_ref,
                  c_ref, n_ref, m_ref):
    L = CHUNK
    H = ML_HEADS

    @pl.when(pl.program_id(1) == 0)
    def _():
        c_ref[...] = jnp.zeros_like(c_ref)
        n_ref[...] = jnp.zeros_like(n_ref)
        m_ref[...] = jnp.zeros_like(m_ref)

    row = lax.broadcasted_iota(jnp.int32, (L, L), 0)
    col = lax.broadcasted_iota(jnp.int32, (L, L), 1)
    lower = col <= row
    upper = row <= col
    for h in range(H):
        q = q_ref[0, :, h * ML_DQK:(h + 1) * ML_DQK]
        k = k_ref[0, :, h * ML_DQK:(h + 1) * ML_DQK]
        v = v_ref[0, :, h * ML_DV:(h + 1) * ML_DV]
        og = o_ref[0, :, h * ML_DV:(h + 1) * ML_DV].astype(F32)
        i_col = gcol_ref[0, :, h:h + 1]
        f_col = gcol_ref[0, :, H + h:H + h + 1]
        i_row = grow_ref[0, 0, h:h + 1, :]
        f_row = grow_ref[0, 0, H + h:H + h + 1, :]
        b_col, b_row = _cumsum_col_row(f_col, f_row, lower, upper)
        m_prev = m_ref[h, 0:1, 0:1]
        c_prev = c_ref[h]
        n_prev = n_ref[h, 0:1, :]

        d = jnp.where(lower, b_col - b_row + i_row, -jnp.inf)
        m_inter = b_col + m_prev
        m_t = jnp.maximum(jnp.max(d, axis=1, keepdims=True), m_inter)
        w = jnp.exp(d - m_t)
        sc = _dot_nt(q, k) * w
        a = jnp.exp(m_inter - m_t)
        qf = q.astype(F32)
        num = _dot(sc.astype(BF16), v) + a * _dot(q, c_prev.astype(BF16))
        den = jnp.sum(sc, axis=1, keepdims=True) + a * jnp.sum(qf * n_prev, axis=1, keepdims=True)
        hc = num / jnp.maximum(jnp.abs(den), jnp.exp(-m_t))

        b_last = b_col[L - 1:L, :]
        dl_col = b_last - b_col + i_col
        dl_row = b_last - b_row + i_row
        m_new = jnp.maximum(b_last + m_prev, jnp.max(dl_row, axis=1, keepdims=True))
        ws_col = jnp.exp(dl_col - m_new)
        decay = jnp.exp(b_last + m_prev - m_new)
        wv = (ws_col * v.astype(F32)).astype(BF16)
        c_ref[h] = decay * c_prev + _dot_tn(k, wv)
        n_new = decay * n_prev + jnp.sum(ws_col * k.astype(F32), axis=0, keepdims=True)
        n_ref[h] = jnp.broadcast_to(n_new, n_ref.shape[1:])
        m_ref[h] = jnp.broadcast_to(m_new, m_ref.shape[1:])

        gain = gain_ref[0:1, h * ML_DV:(h + 1) * ML_DV]
        y = _rms(hc, gain) * _sigmoid(og)
        out_ref[0, :, h * ML_DV:(h + 1) * ML_DV] = y.astype(out_ref.dtype)


def _mlstm(p, gcol, grow, head_gain, bsz, s):
    nc = s // CHUNK
    qk_w = ML_HEADS * ML_DQK
    v_w = ML_HEADS * ML_DV
    ng = 2 * ML_HEADS
    return pl.pallas_call(
        _mlstm_kernel,
        grid=(bsz, nc),
        in_specs=[
            pl.BlockSpec((1, CHUNK, qk_w), lambda b, c: (b, c, 0)),
            pl.BlockSpec((1, CHUNK, qk_w), lambda b, c: (b, c, 1)),
            pl.BlockSpec((1, CHUNK, v_w), lambda b, c: (b, c, 1)),
            pl.BlockSpec((1, CHUNK, v_w), lambda b, c: (b, c, 2)),
            pl.BlockSpec((1, CHUNK, ng), lambda b, c: (b, c, 0)),
            pl.BlockSpec((1, 1, ng, CHUNK), lambda b, c: (b, c, 0, 0)),
            pl.BlockSpec((1, v_w), lambda b, c: (0, 0)),
        ],
        out_specs=pl.BlockSpec((1, CHUNK, v_w), lambda b, c: (b, c, 0)),
        out_shape=jax.ShapeDtypeStruct((bsz, s, v_w), BF16),
        scratch_shapes=[
            pltpu.VMEM((ML_HEADS, ML_DQK, ML_DV), F32),
            pltpu.VMEM((ML_HEADS, 8, ML_DQK), F32),
            pltpu.VMEM((ML_HEADS, 8, 128), F32),
        ],
        compiler_params=pltpu.CompilerParams(dimension_semantics=("parallel", "arbitrary")),
        name="mlstm",
    )(p, p, p, p, gcol, grow, head_gain.reshape(1, v_w).astype(F32))


def _unit_lower_inverse(a):
    L = a.shape[0]
    x = -a
    eye = (lax.broadcasted_iota(jnp.int32, (L, L), 0) == lax.broadcasted_iota(jnp.int32, (L, L), 1)).astype(F32)
    p = eye + x
    steps = max(1, (L - 1).bit_length()) - 1
    for _ in range(steps):
        xb = x.astype(BF16)
        x = _dot(xb, xb)
        p = p + _dot(p.astype(BF16), x.astype(BF16))
    return p


def _gdn_kernel(q_ref, k_ref, v_ref, z_ref, gcol_ref, grow_ref, cw_ref, gain_ref, out_ref,
                s_ref, halo_ref):
    L = CHUNK
    H = GD_HEADS
    kw = H * GD_DK

    @pl.when(pl.program_id(1) == 0)
    def _():
        s_ref[...] = jnp.zeros_like(s_ref)
        halo_ref[...] = jnp.zeros_like(halo_ref)

    row = lax.broadcasted_iota(jnp.int32, (L, L), 0)
    col = lax.broadcasted_iota(jnp.int32, (L, L), 1)
    lower = col <= row
    upper = row <= col
    strict = col < row
    row8 = lax.broadcasted_iota(jnp.int32, (8, 1), 0)

    def conv_silu(x_ref, part):
        x = x_ref[0].astype(F32)
        width = x.shape[1]
        halo = halo_ref[part, :, 0:width]
        cw = cw_ref[:, part * kw:part * kw + width]
        acc = x * cw[GD_CONV_K - 1:GD_CONV_K, :]
        for sft in range(1, GD_CONV_K):
            r = pltpu.roll(x, sft, 0)
            rh = pltpu.roll(halo, sft, 0)
            head = jnp.where(row8 < sft, rh, r[0:8, :])
            r = jnp.concatenate([head, r[8:, :]], axis=0)
            acc = acc + r * cw[GD_CONV_K - 1 - sft:GD_CONV_K - sft, :]
        halo_ref[part, :, 0:width] = x[L - 8:L, :]
        return _silu(acc)

    qa = conv_silu(q_ref, 0)
    ka = conv_silu(k_ref, 1)
    va = conv_silu(v_ref, 2)

    for h in range(H):
        q = qa[:, h * GD_DK:(h + 1) * GD_DK]
        k = ka[:, h * GD_DK:(h + 1) * GD_DK]
        v = va[:, h * GD_DV:(h + 1) * GD_DV]
        z = z_ref[0, :, h * GD_DV:(h + 1) * GD_DV].astype(F32)
        beta = gcol_ref[0, :, h:h + 1]
        g_col = gcol_ref[0, :, H + h:H + h + 1]
        g_row = grow_ref[0, 0, H + h:H + h + 1, :]
        gc_col, gc_row = _cumsum_col_row(g_col, g_row, lower, upper)

        q = q * lax.rsqrt(jnp.sum(q * q, axis=1, keepdims=True) + EPS) * (GD_DK ** -0.5)
        k = k * lax.rsqrt(jnp.sum(k * k, axis=1, keepdims=True) + EPS)
        gamma = jnp.where(lower, jnp.exp(jnp.where(lower, gc_col - gc_row, 0.0)), 0.0)
        kb = k * beta
        k16 = k.astype(BF16)
        a = jnp.where(strict, _dot_nt(kb.astype(BF16), k16) * gamma, 0.0)
        t = _unit_lower_inverse(a).astype(BF16)
        eg = jnp.exp(gc_col)
        u = _dot(t, (v * beta).astype(BF16))
        w = _dot(t, (kb * eg).astype(BF16))
        attn = _dot_nt(q.astype(BF16), k16) * gamma
        qg = q * eg
        g_last = gc_col[L - 1:L, :]
        kdec = k * jnp.exp(g_last - gc_col)

        s_prev = s_ref[h]
        s16 = s_prev.astype(BF16)
        v_new = u - _dot(w.astype(BF16), s16)
        vn16 = v_new.astype(BF16)
        o = _dot(qg.astype(BF16), s16) + _dot(attn.astype(BF16), vn16)
        s_ref[h] = jnp.exp(g_last) * s_prev + _dot_tn(kdec.astype(BF16), vn16)

        y = _rms(o, gain_ref[...]) * _silu(z)
        out_ref[0, :, h * GD_DV:(h + 1) * GD_DV] = y.astype(out_ref.dtype)


def _gdn(p, gcol, grow, conv_w, out_gain, bsz, s):
    nc = s // CHUNK
    kw = GD_HEADS * GD_DK
    vw = GD_HEADS * GD_DV
    ng = 2 * GD_HEADS
    cc = conv_w.shape[1]
    return pl.pallas_call(
        _gdn_kernel,
        grid=(bsz, nc),
        in_specs=[
            pl.BlockSpec((1, CHUNK, kw), lambda b, c: (b, c, 0)),
            pl.BlockSpec((1, CHUNK, kw), lambda b, c: (b, c, 1)),
            pl.BlockSpec((1, CHUNK, vw), lambda b, c: (b, c, 2)),
            pl.BlockSpec((1, CHUNK, vw), lambda b, c: (b, c, 3)),
            pl.BlockSpec((1, CHUNK, ng), lambda b, c: (b, c, 0)),
            pl.BlockSpec((1, 1, ng, CHUNK), lambda b, c: (b, c, 0, 0)),
            pl.BlockSpec((GD_CONV_K, cc), lambda b, c: (0, 0)),
            pl.BlockSpec((1, GD_DV), lambda b, c: (0, 0)),
        ],
        out_specs=pl.BlockSpec((1, CHUNK, vw), lambda b, c: (b, c, 0)),
        out_shape=jax.ShapeDtypeStruct((bsz, s, vw), BF16),
        scratch_shapes=[
            pltpu.VMEM((GD_HEADS, GD_DK, GD_DV), F32),
            pltpu.VMEM((3, 8, kw), F32),
        ],
        compiler_params=pltpu.CompilerParams(dimension_semantics=("parallel", "arbitrary")),
        name="gdn",
    )(p, p, p, p, gcol, grow, conv_w.astype(F32), out_gain.reshape(1, GD_DV).astype(F32))


def _ffn_kernel(x_ref, g_ref, wg_ref, wu_ref, wd_ref, out_ref, h_ref, acc_ref):
    j = pl.program_id(1)

    @pl.when(j == 0)
    def _():
        h_ref[...] = _rms(x_ref[...], g_ref[...]).astype(BF16)
        acc_ref[...] = jnp.zeros_like(acc_ref)

    h = h_ref[...]
    a = _silu(_dot(h, wg_ref[...])) * _dot(h, wu_ref[...])
    acc_ref[...] += _dot(a.astype(BF16), wd_ref[...])

    @pl.when(j == pl.num_programs(1) - 1)
    def _():
        out_ref[...] = x_ref[...] + acc_ref[...]


def _ffn(x2, gain, w_gate, w_up, w_down, *, tm, tf):
    n, d = x2.shape
    f = w_gate.shape[1]
    return pl.pallas_call(
        _ffn_kernel,
        grid=(n // tm, f // tf),
        in_specs=[
            pl.BlockSpec((tm, d), lambda i, j: (i, 0)),
            pl.BlockSpec((1, d), lambda i, j: (0, 0)),
            pl.BlockSpec((d, tf), lambda i, j: (0, j)),
            pl.BlockSpec((d, tf), lambda i, j: (0, j)),
            pl.BlockSpec((tf, d), lambda i, j: (j, 0)),
        ],
        out_specs=pl.BlockSpec((tm, d), lambda i, j: (i, 0)),
        out_shape=jax.ShapeDtypeStruct((n, d), F32),
        scratch_shapes=[pltpu.VMEM((tm, d), BF16), pltpu.VMEM((tm, d), F32)],
        compiler_params=pltpu.CompilerParams(dimension_semantics=("parallel", "arbitrary")),
        name="ffn",
    )(x2, gain.reshape(1, d), w_gate.astype(BF16), w_up.astype(BF16), w_down.astype(BF16))


def _expert_kernel(blk_ref, exp_ref, nvalid_ref, x_ref, wg_ref, wu_ref, wd_ref, out_ref, h_ref, acc_ref):
    del blk_ref, exp_ref
    i = pl.program_id(0)
    j = pl.program_id(1)

    @pl.when(i < nvalid_ref[0])
    def _():
        @pl.when(j == 0)
        def _():
            h_ref[...] = x_ref[...].astype(BF16)
            acc_ref[...] = jnp.zeros_like(acc_ref)

        h = h_ref[...]
        a = _silu(_dot(h, wg_ref[0])) * _dot(h, wu_ref[0])
        acc_ref[...] += _dot(a.astype(BF16), wd_ref[0])

        @pl.when(j == pl.num_programs(1) - 1)
        def _():
            out_ref[...] = acc_ref[...]

    @pl.when(i >= nvalid_ref[0])
    def _():
        out_ref[...] = jnp.zeros_like(out_ref)


def _experts(xs, tile_blk, tile_exp, n_valid, w_gate, w_up, w_down, *, tm, tf):
    m, d = xs.shape
    f = w_gate.shape[2]
    nt = m // tm
    nf = f // tf

    def col(i, j, nv):
        return jnp.where(i < nv[0], j, nf - 1)

    grid_spec = pltpu.PrefetchScalarGridSpec(
        num_scalar_prefetch=3,
        grid=(nt, nf),
        in_specs=[
            pl.BlockSpec((tm, d), lambda i, j, blk, ex, nv: (blk[i], 0)),
            pl.BlockSpec((1, d, tf), lambda i, j, blk, ex, nv: (ex[i], 0, col(i, j, nv))),
            pl.BlockSpec((1, d, tf), lambda i, j, blk, ex, nv: (ex[i], 0, col(i, j, nv))),
            pl.BlockSpec((1, tf, d), lambda i, j, blk, ex, nv: (ex[i], col(i, j, nv), 0)),
        ],
        out_specs=pl.BlockSpec((tm, d), lambda i, j, blk, ex, nv: (i, 0)),
        scratch_shapes=[pltpu.VMEM((tm, d), BF16), pltpu.VMEM((tm, d), F32)],
    )
    return pl.pallas_call(
        _expert_kernel,
        grid_spec=grid_spec,
        out_shape=jax.ShapeDtypeStruct((m, d), F32),
        compiler_params=pltpu.CompilerParams(dimension_semantics=("arbitrary", "arbitrary")),
        name="experts",
    )(tile_blk, tile_exp, n_valid, xs, w_gate.astype(BF16), w_up.astype(BF16), w_down.astype(BF16))


def _router_kernel(x_ref, g_ref, rt_ref, h_ref, idx_ref, gate_ref, rank_ref, cnt_ref, carry_ref):
    tm = x_ref.shape[0]
    ne = rt_ref.shape[0]

    @pl.when(pl.program_id(0) == 0)
    def _():
        carry_ref[...] = jnp.zeros_like(carry_ref)

    h = _rms(x_ref[...], g_ref[...])
    h_ref[...] = h
    logits = _dot_nt(rt_ref[...], h.astype(BF16))
    eidx = lax.broadcasted_iota(jnp.int32, logits.shape, 0)
    m1 = jnp.max(logits, axis=0, keepdims=True)
    i1 = jnp.min(jnp.where(logits == m1, eidx, ne), axis=0, keepdims=True)
    rest = jnp.where(eidx == i1, -jnp.inf, logits)
    m2 = jnp.max(rest, axis=0, keepdims=True)
    i2 = jnp.min(jnp.where(rest == m2, eidx, ne), axis=0, keepdims=True)
    e2 = jnp.exp(m2 - m1)
    den = 1.0 + e2
    idx_ref[...] = jnp.concatenate([i1, i2], axis=0)
    gate_ref[...] = jnp.concatenate([1.0 / den, e2 / den], axis=0)

    sel1 = eidx == i1
    sel2 = eidx == i2
    member = jnp.where(sel1 | sel2, 1.0, 0.0)
    before = (lax.broadcasted_iota(jnp.int32, (tm, tm), 0) < lax.broadcasted_iota(jnp.int32, (tm, tm), 1))
    excl = _dot(member.astype(BF16), jnp.where(before, 1.0, 0.0).astype(BF16))
    excl = excl + carry_ref[:, 0:1]
    r1 = jnp.sum(jnp.where(sel1, excl, 0.0), axis=0, keepdims=True)
    r2 = jnp.sum(jnp.where(sel2, excl, 0.0), axis=0, keepdims=True)
    rank_ref[...] = jnp.concatenate([r1, r2], axis=0).astype(jnp.int32)
    total = carry_ref[:, 0:1] + jnp.sum(member, axis=1, keepdims=True)
    carry_ref[...] = jnp.broadcast_to(total, carry_ref.shape)
    cnt_ref[...] = jnp.broadcast_to(total, cnt_ref.shape).astype(jnp.int32)


def _router(x2, gain, router, *, tm):
    n, d = x2.shape
    ne = router.shape[1]
    return pl.pallas_call(
        _router_kernel,
        grid=(n // tm,),
        in_specs=[
            pl.BlockSpec((tm, d), lambda i: (i, 0)),
            pl.BlockSpec((1, d), lambda i: (0, 0)),
            pl.BlockSpec((ne, d), lambda i: (0, 0)),
        ],
        out_specs=[
            pl.BlockSpec((tm, d), lambda i: (i, 0)),
            pl.BlockSpec((TOP_K, tm), lambda i: (0, i)),
            pl.BlockSpec((TOP_K, tm), lambda i: (0, i)),
            pl.BlockSpec((TOP_K, tm), lambda i: (0, i)),
            pl.BlockSpec((ne, 128), lambda i: (0, 0)),
        ],
        out_shape=[
            jax.ShapeDtypeStruct((n, d), F32),
            jax.ShapeDtypeStruct((TOP_K, n), jnp.int32),
            jax.ShapeDtypeStruct((TOP_K, n), F32),
            jax.ShapeDtypeStruct((TOP_K, n), jnp.int32),
            jax.ShapeDtypeStruct((ne, 128), jnp.int32),
        ],
        scratch_shapes=[pltpu.VMEM((ne, 128), F32)],
        compiler_params=pltpu.CompilerParams(dimension_semantics=("arbitrary",)),
        name="router",
    )(x2, gain.reshape(1, d), router.T.astype(BF16))


def _scatter_kernel(pos_ref, h_ref, xs_in_ref, xs_ref, sem):
    del xs_in_ref
    tm = h_ref.shape[0]

    def row_copy(r, slot):
        dst = pos_ref[0, slot, r]
        return pltpu.make_async_copy(h_ref.at[pl.ds(r, 1)], xs_ref.at[pl.ds(dst, 1)], sem)

    def start(r, carry):
        for slot in range(TOP_K):
            row_copy(r, slot).start()
        return carry

    def wait(r, carry):
        for slot in range(TOP_K):
            row_copy(r, slot).wait()
        return carry

    lax.fori_loop(0, tm, start, 0, unroll=8)
    lax.fori_loop(0, tm, wait, 0, unroll=8)


def _scatter_rows(h, pos_tiles, xs_init, *, tm):
    n, d = h.shape
    return pl.pallas_call(
        _scatter_kernel,
        grid=(n // tm,),
        in_specs=[
            pl.BlockSpec((1, TOP_K, tm), lambda i: (i, 0, 0), memory_space=pltpu.SMEM),
            pl.BlockSpec((tm, d), lambda i: (i, 0)),
            pl.BlockSpec(memory_space=pl.ANY),
        ],
        out_specs=pl.BlockSpec(memory_space=pl.ANY),
        out_shape=jax.ShapeDtypeStruct(xs_init.shape, xs_init.dtype),
        scratch_shapes=[pltpu.SemaphoreType.DMA(())],
        input_output_aliases={2: 0},
        compiler_params=pltpu.CompilerParams(dimension_semantics=("arbitrary",), has_side_effects=True),
        name="scatter_rows",
    )(pos_tiles, h, xs_init)


def _combine_kernel(pos_ref, x_ref, gate_ref, g_ref, ys_ref, out_ref, rows_ref, sem):
    tm = x_ref.shape[0]

    def row_copy(r, slot):
        src = pos_ref[0, slot, r]
        return pltpu.make_async_copy(ys_ref.at[pl.ds(src, 1)], rows_ref.at[slot, pl.ds(r, 1)], sem)

    def start(r, carry):
        for slot in range(TOP_K):
            row_copy(r, slot).start()
        return carry

    def wait(r, carry):
        for slot in range(TOP_K):
            row_copy(r, slot).wait()
        return carry

    lax.fori_loop(0, tm, start, 0, unroll=8)
    lax.fori_loop(0, tm, wait, 0, unroll=8)
    y = x_ref[...] + gate_ref[:, 0:1] * rows_ref[0] + gate_ref[:, 1:2] * rows_ref[1]
    out_ref[...] = _rms(y, g_ref[...])


def _combine(x2, pos_tiles, gate_col, gain, ys, *, tm):
    n, d = x2.shape
    return pl.pallas_call(
        _combine_kernel,
        grid=(n // tm,),
        in_specs=[
            pl.BlockSpec((1, TOP_K, tm), lambda i: (i, 0, 0), memory_space=pltpu.SMEM),
            pl.BlockSpec((tm, d), lambda i: (i, 0)),
            pl.BlockSpec((tm, TOP_K), lambda i: (i, 0)),
            pl.BlockSpec((1, d), lambda i: (0, 0)),
            pl.BlockSpec(memory_space=pl.ANY),
        ],
        out_specs=pl.BlockSpec((tm, d), lambda i: (i, 0)),
        out_shape=jax.ShapeDtypeStruct((n, d), F32),
        scratch_shapes=[pltpu.VMEM((TOP_K, tm, d), F32), pltpu.SemaphoreType.DMA(())],
        compiler_params=pltpu.CompilerParams(dimension_semantics=("arbitrary",)),
        name="combine",
    )(pos_tiles, x2, gate_col, gain.reshape(1, d), ys)


def _moe_and_final_norm(x2, norm_gain, router, w_gate, w_up, w_down, final_gain, *, tm_route, tm_exp, tf):
    n, d = x2.shape
    ne = router.shape[1]
    h, idx_t, gate_t, rank_t, counts = _router(x2, norm_gain, router, tm=tm_route)

    cnt = counts[:, 0]
    padded = ((cnt + tm_exp - 1) // tm_exp) * tm_exp
    ends = jnp.cumsum(padded)
    starts = ends - padded
    pos = starts[idx_t] + rank_t
    pos_tiles = pos.reshape(TOP_K, n // tm_route, tm_route).transpose(1, 0, 2)
    m_rows = TOP_K * n + ne * tm_exp
    nt = m_rows // tm_exp
    n_valid = (ends[-1] // tm_exp).astype(jnp.int32)
    tile_blk = jnp.minimum(jnp.arange(nt, dtype=jnp.int32), n_valid - 1)
    tile_exp = jnp.minimum(
        jnp.searchsorted(ends, tile_blk * tm_exp, side="right").astype(jnp.int32), ne - 1)

    xs = _scatter_rows(h, pos_tiles, jnp.zeros((m_rows, d), F32), tm=tm_route)
    ys = _experts(xs, tile_blk, tile_exp, n_valid.reshape(1), w_gate, w_up, w_down, tm=tm_exp, tf=tf)
    return _combine(x2, pos_tiles, gate_t.T, final_gain, ys, tm=tm_route)


def _trunk(x, l0_norm_mix, l0_ml_w_in, l0_ml_b_if, l0_ml_head_gain, l0_ml_w_out,
           l0_norm_ffn, l0_ffn_w_gate, l0_ffn_w_up, l0_ffn_w_down,
           l1_norm_mix, l1_gdn_w_in, l1_gdn_conv, l1_gdn_a_log, l1_gdn_dt_bias,
           l1_gdn_out_gain, l1_gdn_w_out, l1_norm_ffn, l1_moe_router,
           l1_moe_w_gate, l1_moe_w_up, l1_moe_w_down, final_norm, *, tiles):
    bsz, s, d = x.shape
    n = bsz * s
    nc = s // CHUNK
    x2 = x.reshape(n, d)

    qk_w = ML_HEADS * ML_DQK
    v_w = ML_HEADS * ML_DV
    fm = 2 * qk_w + 2 * v_w
    col_scale = jnp.concatenate([jnp.ones((qk_w,), F32), jnp.full((qk_w,), ML_DQK ** -0.5, F32),
                                 jnp.ones((2 * v_w,), F32)])
    ng = 2 * ML_HEADS
    p, gcol, grow = _inproj(x2, l0_norm_mix, l0_ml_w_in[:, :fm] * col_scale, l0_ml_w_in[:, fm:],
                            l0_ml_b_if, jnp.zeros((ng,), F32), _mlstm_gate_act,
                            tm=tiles["tm_proj"], tn=tiles["tn_proj"])
    grow = grow.reshape(ng, bsz, nc, CHUNK).transpose(1, 2, 0, 3)
    hs = _mlstm(p.reshape(bsz, s, fm), gcol.reshape(bsz, s, ng), grow, l0_ml_head_gain, bsz, s)
    x2 = _outproj(hs.reshape(n, v_w), l0_ml_w_out, x2, tm=tiles["tm_out"])

    x2 = _ffn(x2, l0_norm_ffn, l0_ffn_w_gate, l0_ffn_w_up, l0_ffn_w_down, tm=tiles["tm_ffn"], tf=tiles["tf"])

    kw = GD_HEADS * GD_DK
    vw = GD_HEADS * GD_DV
    fm = 2 * kw + 2 * vw
    ng = 2 * GD_HEADS
    zeros_h = jnp.zeros((GD_HEADS,), F32)
    p, gcol, grow = _inproj(x2, l1_norm_mix, l1_gdn_w_in[:, :fm], l1_gdn_w_in[:, fm:],
                            jnp.concatenate([zeros_h, l1_gdn_dt_bias]),
                            jnp.concatenate([zeros_h, l1_gdn_a_log]), _gdn_gate_act,
                            tm=tiles["tm_proj"], tn=tiles["tn_proj"])
    grow = grow.reshape(ng, bsz, nc, CHUNK).transpose(1, 2, 0, 3)
    o = _gdn(p.reshape(bsz, s, fm), gcol.reshape(bsz, s, ng), grow, l1_gdn_conv, l1_gdn_out_gain, bsz, s)
    x2 = _outproj(o.reshape(n, vw), l1_gdn_w_out, x2, tm=tiles["tm_out"])

    out = _moe_and_final_norm(x2, l1_norm_ffn, l1_moe_router, l1_moe_w_gate, l1_moe_w_up, l1_moe_w_down,
                              final_norm, tm_route=tiles["tm_route"], tm_exp=tiles["tm_exp"], tf=tiles["tf"])
    return out.reshape(bsz, s, d)


def _tiles_for(n):
    cap = lambda t: min(t, n)
    return dict(tm_proj=cap(512), tn_proj=1024, tm_out=cap(512), tm_ffn=cap(1024), tf=512,
                tm_route=cap(512), tm_exp=cap(512))


def kernel(x, l0_norm_mix, l0_ml_w_in, l0_ml_b_if, l0_ml_head_gain, l0_ml_w_out, l0_norm_ffn, l0_ffn_w_gate, l0_ffn_w_up, l0_ffn_w_down, l1_norm_mix, l1_gdn_w_in, l1_gdn_conv, l1_gdn_a_log, l1_gdn_dt_bias, l1_gdn_out_gain, l1_gdn_w_out, l1_norm_ffn, l1_moe_router, l1_moe_w_gate, l1_moe_w_up, l1_moe_w_down, final_norm):
    n = x.shape[0] * x.shape[1]
    return _trunk(x, l0_norm_mix, l0_ml_w_in, l0_ml_b_if, l0_ml_head_gain, l0_ml_w_out,
                  l0_norm_ffn, l0_ffn_w_gate, l0_ffn_w_up, l0_ffn_w_down,
                  l1_norm_mix, l1_gdn_w_in, l1_gdn_conv, l1_gdn_a_log, l1_gdn_dt_bias,
                  l1_gdn_out_gain, l1_gdn_w_out, l1_norm_ffn, l1_moe_router,
                  l1_moe_w_gate, l1_moe_w_up, l1_moe_w_down, final_norm, tiles=_tiles_for(n))
```

```python
import functools

import jax
import jax.numpy as jnp
from jax import lax
from jax.experimental import pallas as pl
from jax.experimental.pallas import tpu as pltpu

EPS = 1e-6
CHUNK = 64

ML_HEADS, ML_DQK, ML_DV = 4, 128, 256
GD_HEADS, GD_DK, GD_DV = 8, 128, 128
GD_CONV_K = 4
N_EXPERTS, TOP_K = 8, 2

F32 = jnp.float32
BF16 = jnp.bfloat16

_NT = (((1,), (1,)), ((), ()))
_TN = (((0,), (0,)), ((), ()))


def _dot(a, b):
    return jnp.dot(a, b, preferred_element_type=F32)


def _dot_nt(a, b):
    return lax.dot_general(a, b, _NT, preferred_element_type=F32)


def _dot_tn(a, b):
    return lax.dot_general(a, b, _TN, preferred_element_type=F32)


def _rms(x, g):
    return x * lax.rsqrt(jnp.mean(x * x, axis=-1, keepdims=True) + EPS) * g


def _softplus(x):
    return jnp.maximum(x, 0.0) + jnp.log(1.0 + jnp.exp(-jnp.abs(x)))


def _sigmoid(x):
    return 1.0 / (1.0 + jnp.exp(-x))


def _silu(x):
    return x * _sigmoid(x)


def _mlstm_gate_act(pre, gidx, p1, p2):
    del p2
    z = pre + p1
    return jnp.where(gidx < ML_HEADS, z, -_softplus(-z))


def _gdn_gate_act(pre, gidx, p1, p2):
    beta = _sigmoid(pre)
    g = -(jnp.exp(p2) * _softplus(pre + p1))
    return jnp.where(gidx < GD_HEADS, beta, g)


def _inproj_kernel(x_ref, g_ref, w_ref, wg_ref, wgt_ref, pc_ref, pr_ref,
                   main_ref, gcol_ref, grow_ref, *, act, tn):
    h = _rms(x_ref[...], g_ref[...]).astype(BF16)
    pre_c = _dot(h, wg_ref[...])
    gi_c = lax.broadcasted_iota(jnp.int32, pre_c.shape, 1)
    gcol_ref[...] = act(pre_c, gi_c, pc_ref[0:1, :], pc_ref[1:2, :])
    pre_r = _dot_nt(wgt_ref[...], h)
    gi_r = lax.broadcasted_iota(jnp.int32, pre_r.shape, 0)
    grow_ref[...] = act(pre_r, gi_r, pr_ref[:, 0:1], pr_ref[:, 1:2])
    for c in range(0, w_ref.shape[1], tn):
        main_ref[:, c:c + tn] = _dot(h, w_ref[:, c:c + tn]).astype(main_ref.dtype)


def _inproj(x2, gain, w_main, w_gate, p1, p2, act, *, tm, tn):
    n, d = x2.shape
    fm = w_main.shape[1]
    ng = w_gate.shape[1]
    pc = jnp.stack([p1, p2]).astype(F32)
    pr = pc.T
    return pl.pallas_call(
        functools.partial(_inproj_kernel, act=act, tn=tn),
        grid=(n // tm,),
        in_specs=[
            pl.BlockSpec((tm, d), lambda i: (i, 0)),
            pl.BlockSpec((1, d), lambda i: (0, 0)),
            pl.BlockSpec((d, fm), lambda i: (0, 0)),
            pl.BlockSpec((d, ng), lambda i: (0, 0)),
            pl.BlockSpec((ng, d), lambda i: (0, 0)),
            pl.BlockSpec((2, ng), lambda i: (0, 0)),
            pl.BlockSpec((ng, 2), lambda i: (0, 0)),
        ],
        out_specs=[
            pl.BlockSpec((tm, fm), lambda i: (i, 0)),
            pl.BlockSpec((tm, ng), lambda i: (i, 0)),
            pl.BlockSpec((ng, tm), lambda i: (0, i)),
        ],
        out_shape=[
            jax.ShapeDtypeStruct((n, fm), BF16),
            jax.ShapeDtypeStruct((n, ng), F32),
            jax.ShapeDtypeStruct((ng, n), F32),
        ],
        compiler_params=pltpu.CompilerParams(dimension_semantics=("parallel",)),
        name="inproj",
    )(x2, gain.reshape(1, d), w_main.astype(BF16), w_gate.astype(BF16), w_gate.T.astype(BF16), pc, pr)


def _cumsum_col_row(v_col, v_row, incl_lower, incl_upper):
    c_col = jnp.sum(jnp.where(incl_lower, v_row, 0.0), axis=1, keepdims=True)
    c_row = jnp.sum(jnp.where(incl_upper, v_col, 0.0), axis=0, keepdims=True)
    return c_col, c_row


def _mlstm_kernel(q_ref, k_ref, v_ref, o_ref, gcol_ref, grow_ref, gain_ref, out_ref,
                  c_ref, n_ref, m_ref):
    L = CHUNK
    H = ML_HEADS

    @pl.when(pl.program_id(1) == 0)
    def _():
        c_ref[...] = jnp.zeros_like(c_ref)
        n_ref[...] = jnp.zeros_like(n_ref)
        m_ref[...] = jnp.zeros_like(m_ref)

    row = lax.broadcasted_iota(jnp.int32, (L, L), 0)
    col = lax.broadcasted_iota(jnp.int32, (L, L), 1)
    lower = col <= row
    upper = row <= col
    units = [(bi, h) for bi in range(q_ref.shape[0]) for h in range(H)]
    hs = range(len(units))
    q = [q_ref[bi, :, h * ML_DQK:(h + 1) * ML_DQK] for bi, h in units]
    k = [k_ref[bi, :, h * ML_DQK:(h + 1) * ML_DQK] for bi, h in units]
    v = [v_ref[bi, :, h * ML_DV:(h + 1) * ML_DV] for bi, h in units]
    i_col = [gcol_ref[bi, :, h:h + 1] for bi, h in units]
    i_row = [grow_ref[bi, 0, h:h + 1, :] for bi, h in units]
    b = [_cumsum_col_row(gcol_ref[bi, :, H + h:H + h + 1], grow_ref[bi, 0, H + h:H + h + 1, :], lower, upper)
         for bi, h in units]
    b_col = [c for c, _ in b]
    b_row = [r for _, r in b]
    m_prev = [m_ref[h, 0:1, 0:1] for h in hs]
    c_prev = [c_ref[h] for h in hs]
    n_prev = [n_ref[h, 0:1, :] for h in hs]

    qk = [_dot_nt(q[h], k[h]) for h in hs]
    qc = [_dot(q[h], c_prev[h].astype(BF16)) for h in hs]
    d = [jnp.where(lower, b_col[h] - b_row[h] + i_row[h], -jnp.inf) for h in hs]
    m_inter = [b_col[h] + m_prev[h] for h in hs]
    m_t = [jnp.maximum(jnp.max(d[h], axis=1, keepdims=True), m_inter[h]) for h in hs]
    sc = [qk[h] * jnp.exp(d[h] - m_t[h]) for h in hs]
    a = [jnp.exp(m_inter[h] - m_t[h]) for h in hs]
    num = [_dot(sc[h].astype(BF16), v[h]) + a[h] * qc[h] for h in hs]
    den = [jnp.sum(sc[h], axis=1, keepdims=True)
           + a[h] * jnp.sum(q[h].astype(F32) * n_prev[h], axis=1, keepdims=True) for h in hs]
    hc = [num[h] / jnp.maximum(jnp.abs(den[h]), jnp.exp(-m_t[h])) for h in hs]

    b_last = [c[L - 1:L, :] for c in b_col]
    m_new = [jnp.maximum(b_last[h] + m_prev[h],
                         jnp.max(b_last[h] - b_row[h] + i_row[h], axis=1, keepdims=True)) for h in hs]
    ws_col = [jnp.exp(b_last[h] - b_col[h] + i_col[h] - m_new[h]) for h in hs]
    decay = [jnp.exp(b_last[h] + m_prev[h] - m_new[h]) for h in hs]
    wv = [(ws_col[h] * v[h].astype(F32)).astype(BF16) for h in hs]
    kv = [_dot_tn(k[h], wv[h]) for h in hs]
    for h in hs:
        c_ref[h] = decay[h] * c_prev[h] + kv[h]
        n_new = decay[h] * n_prev[h] + jnp.sum(ws_col[h] * k[h].astype(F32), axis=0, keepdims=True)
        n_ref[h] = jnp.broadcast_to(n_new, n_ref.shape[1:])
        m_ref[h] = jnp.broadcast_to(m_new[h], m_ref.shape[1:])
    for u, (bi, h) in enumerate(units):
        og = o_ref[bi, :, h * ML_DV:(h + 1) * ML_DV].astype(F32)
        gain = gain_ref[0:1, h * ML_DV:(h + 1) * ML_DV]
        y = _rms(hc[u], gain) * _sigmoid(og)
        out_ref[bi, :, h * ML_DV:(h + 1) * ML_DV] = y.astype(out_ref.dtype)


def _mlstm(p, gcol, grow, head_gain, bsz, s, *, bb):
    nc = s // CHUNK
    qk_w = ML_HEADS * ML_DQK
    v_w = ML_HEADS * ML_DV
    ng = 2 * ML_HEADS
    return pl.pallas_call(
        _mlstm_kernel,
        grid=(bsz // bb, nc),
        in_specs=[
            pl.BlockSpec((bb, CHUNK, qk_w), lambda b, c: (b, c, 0)),
            pl.BlockSpec((bb, CHUNK, qk_w), lambda b, c: (b, c, 1)),
            pl.BlockSpec((bb, CHUNK, v_w), lambda b, c: (b, c, 1)),
            pl.BlockSpec((bb, CHUNK, v_w), lambda b, c: (b, c, 2)),
            pl.BlockSpec((bb, CHUNK, ng), lambda b, c: (b, c, 0)),
            pl.BlockSpec((bb, 1, ng, CHUNK), lambda b, c: (b, c, 0, 0)),
            pl.BlockSpec((1, v_w), lambda b, c: (0, 0)),
        ],
        out_specs=pl.BlockSpec((bb, CHUNK, v_w), lambda b, c: (b, c, 0)),
        out_shape=jax.ShapeDtypeStruct((bsz, s, v_w), BF16),
        scratch_shapes=[
            pltpu.VMEM((bb * ML_HEADS, ML_DQK, ML_DV), F32),
            pltpu.VMEM((bb * ML_HEADS, 8, ML_DQK), F32),
            pltpu.VMEM((bb * ML_HEADS, 8, 128), F32),
        ],
        compiler_params=pltpu.CompilerParams(dimension_semantics=("parallel", "arbitrary")),
        name="mlstm",
    )(p, p, p, p, gcol, grow, head_gain.reshape(1, v_w).astype(F32))


def _unit_lower_inverse_all(a_list):
    L = a_list[0].shape[0]
    eye = (lax.broadcasted_iota(jnp.int32, (L, L), 0) == lax.broadcasted_iota(jnp.int32, (L, L), 1)).astype(F32)
    xs = [-a for a in a_list]
    ps = [eye + x for x in xs]
    steps = max(1, (L - 1).bit_length()) - 1
    for _ in range(steps):
        x16 = [x.astype(BF16) for x in xs]
        xs = [_dot(xb, xb) for xb in x16]
        x16 = [x.astype(BF16) for x in xs]
        ps = [p + _dot(p.astype(BF16), xb) for p, xb in zip(ps, x16)]
    return ps


def _gdn_kernel(q_ref, k_ref, v_ref, z_ref, gcol_ref, grow_ref, cw_ref, gain_ref, out_ref,
                s_ref, halo_ref):
    L = CHUNK
    H = GD_HEADS
    kw = H * GD_DK

    @pl.when(pl.program_id(1) == 0)
    def _():
        s_ref[...] = jnp.zeros_like(s_ref)
        halo_ref[...] = jnp.zeros_like(halo_ref)

    row = lax.broadcasted_iota(jnp.int32, (L, L), 0)
    col = lax.broadcasted_iota(jnp.int32, (L, L), 1)
    lower = col <= row
    upper = row <= col
    strict = col < row
    halo_rows = halo_ref.shape[2]
    nsh = GD_CONV_K - 1
    rr = lax.broadcasted_iota(jnp.int32, (nsh * L, halo_rows + L), 0)
    cc = lax.broadcasted_iota(jnp.int32, (nsh * L, halo_rows + L), 1)
    shifts = jnp.where(cc == halo_rows + (rr % L) - (rr // L + 1), 1.0, 0.0).astype(BF16)

    def conv_silu(x_ref, part, bi):
        xb = x_ref[bi]
        width = xb.shape[1]
        ext = jnp.concatenate([halo_ref[bi, part, :, 0:width], xb], axis=0)
        r = _dot(shifts, ext)
        cw = cw_ref[:, part * kw:part * kw + width]
        acc = xb.astype(F32) * cw[GD_CONV_K - 1:GD_CONV_K, :]
        for sft in range(1, GD_CONV_K):
            acc = acc + r[(sft - 1) * L:sft * L, :] * cw[GD_CONV_K - 1 - sft:GD_CONV_K - sft, :]
        halo_ref[bi, part, :, 0:width] = xb[L - halo_rows:L, :]
        return _silu(acc)

    bb = q_ref.shape[0]
    qa = [conv_silu(q_ref, 0, bi) for bi in range(bb)]
    ka = [conv_silu(k_ref, 1, bi) for bi in range(bb)]
    va = [conv_silu(v_ref, 2, bi) for bi in range(bb)]

    units = [(bi, h) for bi in range(bb) for h in range(H)]
    hs = range(len(units))
    beta = [gcol_ref[bi, :, h:h + 1] for bi, h in units]
    gc = [_cumsum_col_row(gcol_ref[bi, :, H + h:H + h + 1], grow_ref[bi, 0, H + h:H + h + 1, :], lower, upper)
          for bi, h in units]
    gc_col = [c for c, _ in gc]
    gamma = [jnp.where(lower, jnp.exp(jnp.where(lower, c - r, 0.0)), 0.0) for c, r in gc]
    eg = [jnp.exp(c) for c in gc_col]
    g_last = [c[L - 1:L, :] for c in gc_col]

    q = [qa[bi][:, h * GD_DK:(h + 1) * GD_DK] for bi, h in units]
    k = [ka[bi][:, h * GD_DK:(h + 1) * GD_DK] for bi, h in units]
    v = [va[bi][:, h * GD_DV:(h + 1) * GD_DV] for bi, h in units]
    q = [x * lax.rsqrt(jnp.sum(x * x, axis=1, keepdims=True) + EPS) * (GD_DK ** -0.5) for x in q]
    k = [x * lax.rsqrt(jnp.sum(x * x, axis=1, keepdims=True) + EPS) for x in k]
    kb = [k[h] * beta[h] for h in hs]
    k16 = [x.astype(BF16) for x in k]

    kq = [_dot_nt(jnp.concatenate([kb[h], q[h]], axis=0).astype(BF16), k16[h]) for h in hs]
    a = [jnp.where(strict, kq[h][0:L] * gamma[h], 0.0) for h in hs]
    attn = [(kq[h][L:2 * L] * gamma[h]).astype(BF16) for h in hs]
    t = [x.astype(BF16) for x in _unit_lower_inverse_all(a)]
    uw = [_dot(t[h], jnp.concatenate([v[h] * beta[h], kb[h] * eg[h]], axis=1).astype(BF16)) for h in hs]

    s_prev = [s_ref[h] for h in hs]
    s16 = [x.astype(BF16) for x in s_prev]
    ws = [_dot(jnp.concatenate([uw[h][:, GD_DV:], q[h] * eg[h]], axis=0).astype(BF16), s16[h]) for h in hs]
    vn16 = [(uw[h][:, 0:GD_DV] - ws[h][0:L]).astype(BF16) for h in hs]
    o = [ws[h][L:2 * L] + _dot(attn[h], vn16[h]) for h in hs]
    kdec = [(k[h] * jnp.exp(g_last[h] - gc_col[h])).astype(BF16) for h in hs]
    for h in hs:
        s_ref[h] = jnp.exp(g_last[h]) * s_prev[h] + _dot_tn(kdec[h], vn16[h])
    for u, (bi, h) in enumerate(units):
        z = z_ref[bi, :, h * GD_DV:(h + 1) * GD_DV].astype(F32)
        y = _rms(o[u], gain_ref[...]) * _silu(z)
        out_ref[bi, :, h * GD_DV:(h + 1) * GD_DV] = y.astype(out_ref.dtype)


def _gdn(p, gcol, grow, conv_w, out_gain, bsz, s, *, bb):
    nc = s // CHUNK
    kw = GD_HEADS * GD_DK
    vw = GD_HEADS * GD_DV
    ng = 2 * GD_HEADS
    cc = conv_w.shape[1]
    return pl.pallas_call(
        _gdn_kernel,
        grid=(bsz // bb, nc),
        in_specs=[
            pl.BlockSpec((bb, CHUNK, kw), lambda b, c: (b, c, 0)),
            pl.BlockSpec((bb, CHUNK, kw), lambda b, c: (b, c, 1)),
            pl.BlockSpec((bb, CHUNK, vw), lambda b, c: (b, c, 2)),
            pl.BlockSpec((bb, CHUNK, vw), lambda b, c: (b, c, 3)),
            pl.BlockSpec((bb, CHUNK, ng), lambda b, c: (b, c, 0)),
            pl.BlockSpec((bb, 1, ng, CHUNK), lambda b, c: (b, c, 0, 0)),
            pl.BlockSpec((GD_CONV_K, cc), lambda b, c: (0, 0)),
            pl.BlockSpec((1, GD_DV), lambda b, c: (0, 0)),
        ],
        out_specs=pl.BlockSpec((bb, CHUNK, vw), lambda b, c: (b, c, 0)),
        out_shape=jax.ShapeDtypeStruct((bsz, s, vw), BF16),
        scratch_shapes=[
            pltpu.VMEM((bb * GD_HEADS, GD_DK, GD_DV), F32),
            pltpu.VMEM((bb, 3, 16, kw), BF16),
        ],
        compiler_params=pltpu.CompilerParams(dimension_semantics=("parallel", "arbitrary")),
        name="gdn",
    )(p, p, p, p, gcol, grow, conv_w.astype(F32), out_gain.reshape(1, GD_DV).astype(F32))


def _ffn_kernel(a_ref, wo_ref, res_ref, g_ref, wg_ref, wu_ref, wd_ref, out_ref, h_ref, acc_ref):
    j = pl.program_id(1)

    @pl.when(j == 0)
    def _():
        x1 = res_ref[...] + _dot(a_ref[...], wo_ref[...])
        h_ref[...] = _rms(x1, g_ref[...]).astype(BF16)
        acc_ref[...] = x1

    h = h_ref[...]
    a = _silu(_dot(h, wg_ref[...])) * _dot(h, wu_ref[...])
    acc_ref[...] += _dot(a.astype(BF16), wd_ref[...])

    @pl.when(j == pl.num_programs(1) - 1)
    def _():
        out_ref[...] = acc_ref[...]


def _outproj_ffn(a, w_out, res, gain, w_gate, w_up, w_down, *, tm, tf):
    n, d = res.shape
    k = a.shape[1]
    f = w_gate.shape[1]
    return pl.pallas_call(
        _ffn_kernel,
        grid=(n // tm, f // tf),
        in_specs=[
            pl.BlockSpec((tm, k), lambda i, j: (i, 0)),
            pl.BlockSpec((k, d), lambda i, j: (0, 0)),
            pl.BlockSpec((tm, d), lambda i, j: (i, 0)),
            pl.BlockSpec((1, d), lambda i, j: (0, 0)),
            pl.BlockSpec((d, tf), lambda i, j: (0, j)),
            pl.BlockSpec((d, tf), lambda i, j: (0, j)),
            pl.BlockSpec((tf, d), lambda i, j: (j, 0)),
        ],
        out_specs=pl.BlockSpec((tm, d), lambda i, j: (i, 0)),
        out_shape=jax.ShapeDtypeStruct((n, d), F32),
        scratch_shapes=[pltpu.VMEM((tm, d), BF16), pltpu.VMEM((tm, d), F32)],
        compiler_params=pltpu.CompilerParams(dimension_semantics=("parallel", "arbitrary")),
        name="ffn",
    )(a, w_out.astype(BF16), res, gain.reshape(1, d), w_gate.astype(BF16), w_up.astype(BF16),
      w_down.astype(BF16))


def _expert_kernel(blk_ref, exp_ref, nvalid_ref, x_ref, wg_ref, wu_ref, wd_ref, out_ref, h_ref, acc_ref):
    del blk_ref, exp_ref
    i = pl.program_id(0)
    j = pl.program_id(1)

    @pl.when(i < nvalid_ref[0])
    def _():
        @pl.when(j == 0)
        def _():
            h_ref[...] = x_ref[...].astype(BF16)
            acc_ref[...] = jnp.zeros_like(acc_ref)

        h = h_ref[...]
        a = _silu(_dot(h, wg_ref[0])) * _dot(h, wu_ref[0])
        acc_ref[...] += _dot(a.astype(BF16), wd_ref[0])

        @pl.when(j == pl.num_programs(1) - 1)
        def _():
            out_ref[...] = acc_ref[...]

    @pl.when(i >= nvalid_ref[0])
    def _():
        out_ref[...] = jnp.zeros_like(out_ref)


def _experts(xs, tile_blk, tile_exp, n_valid, w_gate, w_up, w_down, *, tm, tf):
    m, d = xs.shape
    f = w_gate.shape[2]
    nt = m // tm
    nf = f // tf

    def col(i, j, nv):
        return jnp.where(i < nv[0], j, nf - 1)

    grid_spec = pltpu.PrefetchScalarGridSpec(
        num_scalar_prefetch=3,
        grid=(nt, nf),
        in_specs=[
            pl.BlockSpec((tm, d), lambda i, j, blk, ex, nv: (blk[i], 0)),
            pl.BlockSpec((1, d, tf), lambda i, j, blk, ex, nv: (ex[i], 0, col(i, j, nv))),
            pl.BlockSpec((1, d, tf), lambda i, j, blk, ex, nv: (ex[i], 0, col(i, j, nv))),
            pl.BlockSpec((1, tf, d), lambda i, j, blk, ex, nv: (ex[i], col(i, j, nv), 0)),
        ],
        out_specs=pl.BlockSpec((tm, d), lambda i, j, blk, ex, nv: (i, 0)),
        scratch_shapes=[pltpu.VMEM((tm, d), BF16), pltpu.VMEM((tm, d), F32)],
    )
    return pl.pallas_call(
        _expert_kernel,
        grid_spec=grid_spec,
        out_shape=jax.ShapeDtypeStruct((m, d), F32),
        compiler_params=pltpu.CompilerParams(dimension_semantics=("arbitrary", "arbitrary")),
        name="experts",
    )(tile_blk, tile_exp, n_valid, xs, w_gate.astype(BF16), w_up.astype(BF16), w_down.astype(BF16))


def _router_kernel(a_ref, wo_ref, res_ref, g_ref, rt_ref,
                   x_ref, h_ref, idx_ref, gate_ref, rank_ref, cnt_ref, carry_ref):
    tm = res_ref.shape[0]
    ne = rt_ref.shape[0]

    @pl.when(pl.program_id(0) == 0)
    def _():
        carry_ref[...] = jnp.zeros_like(carry_ref)

    x = res_ref[...] + _dot(a_ref[...], wo_ref[...])
    x_ref[...] = x
    h = _rms(x, g_ref[...])
    h_ref[...] = h
    logits = _dot_nt(rt_ref[...], h.astype(BF16))
    eidx = lax.broadcasted_iota(jnp.int32, logits.shape, 0)
    m1 = jnp.max(logits, axis=0, keepdims=True)
    i1 = jnp.min(jnp.where(logits == m1, eidx, ne), axis=0, keepdims=True)
    rest = jnp.where(eidx == i1, -jnp.inf, logits)
    m2 = jnp.max(rest, axis=0, keepdims=True)
    i2 = jnp.min(jnp.where(rest == m2, eidx, ne), axis=0, keepdims=True)
    e2 = jnp.exp(m2 - m1)
    den = 1.0 + e2
    idx_ref[...] = jnp.concatenate([i1, i2], axis=0)
    gate_ref[...] = jnp.concatenate([1.0 / den, e2 / den], axis=0)

    sel1 = eidx == i1
    sel2 = eidx == i2
    member = jnp.where(sel1 | sel2, 1.0, 0.0)
    before = (lax.broadcasted_iota(jnp.int32, (tm, tm), 0) < lax.broadcasted_iota(jnp.int32, (tm, tm), 1))
    excl = _dot(member.astype(BF16), jnp.where(before, 1.0, 0.0).astype(BF16))
    excl = excl + carry_ref[:, 0:1]
    r1 = jnp.sum(jnp.where(sel1, excl, 0.0), axis=0, keepdims=True)
    r2 = jnp.sum(jnp.where(sel2, excl, 0.0), axis=0, keepdims=True)
    rank_ref[...] = jnp.concatenate([r1, r2], axis=0).astype(jnp.int32)
    total = carry_ref[:, 0:1] + jnp.sum(member, axis=1, keepdims=True)
    carry_ref[...] = jnp.broadcast_to(total, carry_ref.shape)
    cnt_ref[...] = jnp.broadcast_to(total, cnt_ref.shape).astype(jnp.int32)


def _outproj_router(a, w_out, res, gain, router, *, tm):
    n, d = res.shape
    k = a.shape[1]
    ne = router.shape[1]
    return pl.pallas_call(
        _router_kernel,
        grid=(n // tm,),
        in_specs=[
            pl.BlockSpec((tm, k), lambda i: (i, 0)),
            pl.BlockSpec((k, d), lambda i: (0, 0)),
            pl.BlockSpec((tm, d), lambda i: (i, 0)),
            pl.BlockSpec((1, d), lambda i: (0, 0)),
            pl.BlockSpec((ne, d), lambda i: (0, 0)),
        ],
        out_specs=[
            pl.BlockSpec((tm, d), lambda i: (i, 0)),
            pl.BlockSpec((tm, d), lambda i: (i, 0)),
            pl.BlockSpec((TOP_K, tm), lambda i: (0, i)),
            pl.BlockSpec((TOP_K, tm), lambda i: (0, i)),
            pl.BlockSpec((TOP_K, tm), lambda i: (0, i)),
            pl.BlockSpec((ne, 128), lambda i: (0, 0)),
        ],
        out_shape=[
            jax.ShapeDtypeStruct((n, d), F32),
            jax.ShapeDtypeStruct((n, d), F32),
            jax.ShapeDtypeStruct((TOP_K, n), jnp.int32),
            jax.ShapeDtypeStruct((TOP_K, n), F32),
            jax.ShapeDtypeStruct((TOP_K, n), jnp.int32),
            jax.ShapeDtypeStruct((ne, 128), jnp.int32),
        ],
        scratch_shapes=[pltpu.VMEM((ne, 128), F32)],
        compiler_params=pltpu.CompilerParams(dimension_semantics=("arbitrary",)),
        name="router",
    )(a, w_out.astype(BF16), res, gain.reshape(1, d), router.T.astype(BF16))


def _scatter_kernel(pos_ref, h_ref, xs_in_ref, xs_ref, sem):
    del xs_in_ref
    tm = h_ref.shape[0]

    def row_copy(r, slot):
        dst = pos_ref[0, slot, r]
        return pltpu.make_async_copy(h_ref.at[pl.ds(r, 1)], xs_ref.at[pl.ds(dst, 1)], sem)

    def start(r, carry):
        for slot in range(TOP_K):
            row_copy(r, slot).start()
        return carry

    def wait(r, carry):
        for slot in range(TOP_K):
            row_copy(r, slot).wait()
        return carry

    lax.fori_loop(0, tm, start, 0, unroll=8)
    lax.fori_loop(0, tm, wait, 0, unroll=8)


def _scatter_rows(h, pos_tiles, xs_init, *, tm):
    n, d = h.shape
    return pl.pallas_call(
        _scatter_kernel,
        grid=(n // tm,),
        in_specs=[
            pl.BlockSpec((1, TOP_K, tm), lambda i: (i, 0, 0), memory_space=pltpu.SMEM),
            pl.BlockSpec((tm, d), lambda i: (i, 0)),
            pl.BlockSpec(memory_space=pl.ANY),
        ],
        out_specs=pl.BlockSpec(memory_space=pl.ANY),
        out_shape=jax.ShapeDtypeStruct(xs_init.shape, xs_init.dtype),
        scratch_shapes=[pltpu.SemaphoreType.DMA(())],
        input_output_aliases={2: 0},
        compiler_params=pltpu.CompilerParams(dimension_semantics=("arbitrary",), has_side_effects=True),
        name="scatter_rows",
    )(pos_tiles, h, xs_init)


def _combine_kernel(pos_ref, x_ref, gate_ref, g_ref, ys_ref, out_ref, rows_ref, sem):
    tm = x_ref.shape[0]

    def row_copy(r, slot):
        src = pos_ref[0, slot, r]
        return pltpu.make_async_copy(ys_ref.at[pl.ds(src, 1)], rows_ref.at[slot, pl.ds(r, 1)], sem)

    def start(r, carry):
        for slot in range(TOP_K):
            row_copy(r, slot).start()
        return carry

    def wait(r, carry):
        for slot in range(TOP_K):
            row_copy(r, slot).wait()
        return carry

    lax.fori_loop(0, tm, start, 0, unroll=8)
    lax.fori_loop(0, tm, wait, 0, unroll=8)
    y = x_ref[...] + gate_ref[:, 0:1] * rows_ref[0] + gate_ref[:, 1:2] * rows_ref[1]
    out_ref[...] = _rms(y, g_ref[...])


def _combine(x2, pos_tiles, gate_col, gain, ys, *, tm):
    n, d = x2.shape
    return pl.pallas_call(
        _combine_kernel,
        grid=(n // tm,),
        in_specs=[
            pl.BlockSpec((1, TOP_K, tm), lambda i: (i, 0, 0), memory_space=pltpu.SMEM),
            pl.BlockSpec((tm, d), lambda i: (i, 0)),
            pl.BlockSpec((tm, TOP_K), lambda i: (i, 0)),
            pl.BlockSpec((1, d), lambda i: (0, 0)),
            pl.BlockSpec(memory_space=pl.ANY),
        ],
        out_specs=pl.BlockSpec((tm, d), lambda i: (i, 0)),
        out_shape=jax.ShapeDtypeStruct((n, d), F32),
        scratch_shapes=[pltpu.VMEM((TOP_K, tm, d), F32), pltpu.SemaphoreType.DMA(())],
        compiler_params=pltpu.CompilerParams(dimension_semantics=("arbitrary",)),
        name="combine",
    )(pos_tiles, x2, gate_col, gain.reshape(1, d), ys)


def _moe_and_final_norm(a, w_out, res, norm_gain, router, w_gate, w_up, w_down, final_gain,
                        *, tm_route, tm_exp, tf):
    n, d = res.shape
    ne = router.shape[1]
    x2, h, idx_t, gate_t, rank_t, counts = _outproj_router(a, w_out, res, norm_gain, router, tm=tm_route)

    cnt = counts[:, 0]
    padded = ((cnt + tm_exp - 1) // tm_exp) * tm_exp
    ends = jnp.cumsum(padded)
    starts = ends - padded
    pos = rank_t
    for e in range(ne):
        pos = pos + jnp.where(idx_t == e, starts[e], 0)
    pos_tiles = pos.reshape(TOP_K, n // tm_route, tm_route).transpose(1, 0, 2)
    m_rows = TOP_K * n + ne * tm_exp
    nt = m_rows // tm_exp
    n_valid = (ends[-1] // tm_exp).astype(jnp.int32)
    tile_blk = jnp.minimum(jnp.arange(nt, dtype=jnp.int32), n_valid - 1)
    tile_exp = jnp.minimum(
        jnp.searchsorted(ends, tile_blk * tm_exp, side="right").astype(jnp.int32), ne - 1)

    xs = _scatter_rows(h, pos_tiles, jnp.zeros((m_rows, d), F32), tm=tm_route)
    ys = _experts(xs, tile_blk, tile_exp, n_valid.reshape(1), w_gate, w_up, w_down, tm=tm_exp, tf=tf)
    return _combine(x2, pos_tiles, gate_t.T, final_gain, ys, tm=tm_route)


def _trunk(x, l0_norm_mix, l0_ml_w_in, l0_ml_b_if, l0_ml_head_gain, l0_ml_w_out,
           l0_norm_ffn, l0_ffn_w_gate, l0_ffn_w_up, l0_ffn_w_down,
           l1_norm_mix, l1_gdn_w_in, l1_gdn_conv, l1_gdn_a_log, l1_gdn_dt_bias,
           l1_gdn_out_gain, l1_gdn_w_out, l1_norm_ffn, l1_moe_router,
           l1_moe_w_gate, l1_moe_w_up, l1_moe_w_down, final_norm, *, tiles):
    bsz, s, d = x.shape
    n = bsz * s
    nc = s // CHUNK
    x2 = x.reshape(n, d)

    qk_w = ML_HEADS * ML_DQK
    v_w = ML_HEADS * ML_DV
    fm = 2 * qk_w + 2 * v_w
    col_scale = jnp.concatenate([jnp.ones((qk_w,), F32), jnp.full((qk_w,), ML_DQK ** -0.5, F32),
                                 jnp.ones((2 * v_w,), F32)])
    ng = 2 * ML_HEADS
    p, gcol, grow = _inproj(x2, l0_norm_mix, l0_ml_w_in[:, :fm] * col_scale, l0_ml_w_in[:, fm:],
                            l0_ml_b_if, jnp.zeros((ng,), F32), _mlstm_gate_act,
                            tm=tiles["tm_proj"], tn=tiles["tn_proj"])
    grow = grow.reshape(ng, bsz, nc, CHUNK).transpose(1, 2, 0, 3)
    hs = _mlstm(p.reshape(bsz, s, fm), gcol.reshape(bsz, s, ng), grow, l0_ml_head_gain, bsz, s,
                bb=tiles["bb_ml"])

    x2 = _outproj_ffn(hs.reshape(n, v_w), l0_ml_w_out, x2, l0_norm_ffn, l0_ffn_w_gate, l0_ffn_w_up,
                      l0_ffn_w_down, tm=tiles["tm_ffn"], tf=tiles["tf"])

    kw = GD_HEADS * GD_DK
    vw = GD_HEADS * GD_DV
    fm = 2 * kw + 2 * vw
    ng = 2 * GD_HEADS
    zeros_h = jnp.zeros((GD_HEADS,), F32)
    p, gcol, grow = _inproj(x2, l1_norm_mix, l1_gdn_w_in[:, :fm], l1_gdn_w_in[:, fm:],
                            jnp.concatenate([zeros_h, l1_gdn_dt_bias]),
                            jnp.concatenate([zeros_h, l1_gdn_a_log]), _gdn_gate_act,
                            tm=tiles["tm_proj"], tn=tiles["tn_proj"])
    grow = grow.reshape(ng, bsz, nc, CHUNK).transpose(1, 2, 0, 3)
    o = _gdn(p.reshape(bsz, s, fm), gcol.reshape(bsz, s, ng), grow, l1_gdn_conv, l1_gdn_out_gain, bsz, s,
             bb=tiles["bb_gd"])

    out = _moe_and_final_norm(o.reshape(n, vw), l1_gdn_w_out, x2, l1_norm_ffn, l1_moe_router,
                              l1_moe_w_gate, l1_moe_w_up, l1_moe_w_down, final_norm,
                              tm_route=tiles["tm_route"], tm_exp=tiles["tm_exp"], tf=tiles["tf"])
    return out.reshape(bsz, s, d)


def _tiles_for(n, bsz, f):
    cap = lambda t: min(t, n)
    return dict(tm_proj=cap(512), tn_proj=1024, tm_ffn=cap(512), tf=f // 2,
                tm_route=cap(512), tm_exp=cap(512), bb_ml=min(4, bsz), bb_gd=min(4, bsz))


def kernel(x, l0_norm_mix, l0_ml_w_in, l0_ml_b_if, l0_ml_head_gain, l0_ml_w_out, l0_norm_ffn, l0_ffn_w_gate, l0_ffn_w_up, l0_ffn_w_down, l1_norm_mix, l1_gdn_w_in, l1_gdn_conv, l1_gdn_a_log, l1_gdn_dt_bias, l1_gdn_out_gain, l1_gdn_w_out, l1_norm_ffn, l1_moe_router, l1_moe_w_gate, l1_moe_w_up, l1_moe_w_down, final_norm):
    n = x.shape[0] * x.shape[1]
    return _trunk(x, l0_norm_mix, l0_ml_w_in, l0_ml_b_if, l0_ml_head_gain, l0_ml_w_out,
                  l0_norm_ffn, l0_ffn_w_gate, l0_ffn_w_up, l0_ffn_w_down,
                  l1_norm_mix, l1_gdn_w_in, l1_gdn_conv, l1_gdn_a_log, l1_gdn_dt_bias,
                  l1_gdn_out_gain, l1_gdn_w_out, l1_norm_ffn, l1_moe_router,
                  l1_moe_w_gate, l1_moe_w_up, l1_moe_w_down, final_norm, tiles=_tiles_for(n, x.shape[0], l0_ffn_w_gate.shape[1]))
```

```python
import functools

import jax
import jax.numpy as jnp
from jax import lax
from jax.experimental import pallas as pl
from jax.experimental.pallas import tpu as pltpu

EPS = 1e-6
CHUNK = 64

ML_HEADS, ML_DQK, ML_DV = 4, 128, 256
GD_HEADS, GD_DK, GD_DV = 8, 128, 128
GD_CONV_K = 4
N_EXPERTS, TOP_K = 8, 2

F32 = jnp.float32
BF16 = jnp.bfloat16

_NT = (((1,), (1,)), ((), ()))
_TN = (((0,), (0,)), ((), ()))


def _dot(a, b):
    return jnp.dot(a, b, preferred_element_type=F32)


def _dot_nt(a, b):
    return lax.dot_general(a, b, _NT, preferred_element_type=F32)


def _dot_tn(a, b):
    return lax.dot_general(a, b, _TN, preferred_element_type=F32)


def _rms(x, g):
    return x * lax.rsqrt(jnp.mean(x * x, axis=-1, keepdims=True) + EPS) * g


def _softplus(x):
    return jnp.maximum(x, 0.0) + jnp.log(1.0 + jnp.exp(-jnp.abs(x)))


def _sigmoid(x):
    return 1.0 / (1.0 + jnp.exp(-x))


def _silu(x):
    return x * _sigmoid(x)


LANES = 128


def _rows_to_slabs(ref, x):
    rows, d = x.shape
    dc = d // LANES
    for c in range(dc):
        ref[pl.ds(c, rows, stride=dc), :] = x[:, c * LANES:(c + 1) * LANES]


def _slabs_to_rows(ref, rows):
    dc = ref.shape[0] // rows
    return jnp.concatenate([ref[pl.ds(c, rows, stride=dc), :] for c in range(dc)], axis=1)


def _mlstm_gate_act(pre, gidx, p1, p2):
    del p2
    z = pre + p1
    return jnp.where(gidx < ML_HEADS, z, -_softplus(-z))


def _gdn_gate_act(pre, gidx, p1, p2):
    beta = _sigmoid(pre)
    g = -(jnp.exp(p2) * _softplus(pre + p1))
    return jnp.where(gidx < GD_HEADS, beta, g)


def _inproj_kernel(x_ref, g_ref, w_ref, wg_ref, wgt_ref, pc_ref, pr_ref,
                   main_ref, gcol_ref, grow_ref, *, act, tn):
    h = _rms(x_ref[...], g_ref[...]).astype(BF16)
    pre_c = _dot(h, wg_ref[...])
    gi_c = lax.broadcasted_iota(jnp.int32, pre_c.shape, 1)
    gcol_ref[...] = act(pre_c, gi_c, pc_ref[0:1, :], pc_ref[1:2, :])
    pre_r = _dot_nt(wgt_ref[...], h)
    gi_r = lax.broadcasted_iota(jnp.int32, pre_r.shape, 0)
    grow_ref[...] = act(pre_r, gi_r, pr_ref[:, 0:1], pr_ref[:, 1:2])
    for c in range(0, w_ref.shape[1], tn):
        main_ref[:, c:c + tn] = _dot(h, w_ref[:, c:c + tn]).astype(main_ref.dtype)


def _inproj(x2, gain, w_main, w_gate, p1, p2, act, *, tm, tn):
    n, d = x2.shape
    fm = w_main.shape[1]
    ng = w_gate.shape[1]
    pc = jnp.stack([p1, p2]).astype(F32)
    pr = pc.T
    return pl.pallas_call(
        functools.partial(_inproj_kernel, act=act, tn=tn),
        grid=(n // tm,),
        in_specs=[
            pl.BlockSpec((tm, d), lambda i: (i, 0)),
            pl.BlockSpec((1, d), lambda i: (0, 0)),
            pl.BlockSpec((d, fm), lambda i: (0, 0)),
            pl.BlockSpec((d, ng), lambda i: (0, 0)),
            pl.BlockSpec((ng, d), lambda i: (0, 0)),
            pl.BlockSpec((2, ng), lambda i: (0, 0)),
            pl.BlockSpec((ng, 2), lambda i: (0, 0)),
        ],
        out_specs=[
            pl.BlockSpec((tm, fm), lambda i: (i, 0)),
            pl.BlockSpec((tm, ng), lambda i: (i, 0)),
            pl.BlockSpec((ng, tm), lambda i: (0, i)),
        ],
        out_shape=[
            jax.ShapeDtypeStruct((n, fm), BF16),
            jax.ShapeDtypeStruct((n, ng), F32),
            jax.ShapeDtypeStruct((ng, n), F32),
        ],
        compiler_params=pltpu.CompilerParams(dimension_semantics=("parallel",)),
        name="inproj",
    )(x2, gain.reshape(1, d), w_main.astype(BF16), w_gate.astype(BF16), w_gate.T.astype(BF16), pc, pr)


def _cumsum_col_row(v_col, v_row, incl_lower, incl_upper):
    c_col = jnp.sum(jnp.where(incl_lower, v_row, 0.0), axis=1, keepdims=True)
    c_row = jnp.sum(jnp.where(incl_upper, v_col, 0.0), axis=0, keepdims=True)
    return c_col, c_row


def _mlstm_kernel(q_ref, k_ref, v_ref, o_ref, gcol_ref, grow_ref, gain_ref, out_ref,
                  c_ref, n_ref, m_ref):
    L = CHUNK
    H = ML_HEADS

    @pl.when(pl.program_id(1) == 0)
    def _():
        c_ref[...] = jnp.zeros_like(c_ref)
        n_ref[...] = jnp.zeros_like(n_ref)
        m_ref[...] = jnp.zeros_like(m_ref)

    row = lax.broadcasted_iota(jnp.int32, (L, L), 0)
    col = lax.broadcasted_iota(jnp.int32, (L, L), 1)
    lower = col <= row
    upper = row <= col
    units = [(bi, h) for bi in range(q_ref.shape[0]) for h in range(H)]
    hs = range(len(units))
    q = [q_ref[bi, :, h * ML_DQK:(h + 1) * ML_DQK] for bi, h in units]
    k = [k_ref[bi, :, h * ML_DQK:(h + 1) * ML_DQK] for bi, h in units]
    v = [v_ref[bi, :, h * ML_DV:(h + 1) * ML_DV] for bi, h in units]
    i_col = [gcol_ref[bi, :, h:h + 1] for bi, h in units]
    i_row = [grow_ref[bi, 0, h:h + 1, :] for bi, h in units]
    b = [_cumsum_col_row(gcol_ref[bi, :, H + h:H + h + 1], grow_ref[bi, 0, H + h:H + h + 1, :], lower, upper)
         for bi, h in units]
    b_col = [c for c, _ in b]
    b_row = [r for _, r in b]
    m_prev = [m_ref[h, 0:1, 0:1] for h in hs]
    c_prev = [c_ref[h] for h in hs]
    n_prev = [n_ref[h, 0:1, :] for h in hs]

    qk = [_dot_nt(q[h], k[h]) for h in hs]
    qc = [_dot(q[h], c_prev[h].astype(BF16)) for h in hs]
    d = [jnp.where(lower, b_col[h] - b_row[h] + i_row[h], -jnp.inf) for h in hs]
    m_inter = [b_col[h] + m_prev[h] for h in hs]
    m_t = [jnp.maximum(jnp.max(d[h], axis=1, keepdims=True), m_inter[h]) for h in hs]
    sc = [qk[h] * jnp.exp(d[h] - m_t[h]) for h in hs]
    a = [jnp.exp(m_inter[h] - m_t[h]) for h in hs]
    num = [_dot(sc[h].astype(BF16), v[h]) + a[h] * qc[h] for h in hs]
    den = [jnp.sum(sc[h], axis=1, keepdims=True)
           + a[h] * jnp.sum(q[h].astype(F32) * n_prev[h], axis=1, keepdims=True) for h in hs]
    hc = [num[h] / jnp.maximum(jnp.abs(den[h]), jnp.exp(-m_t[h])) for h in hs]

    b_last = [c[L - 1:L, :] for c in b_col]
    m_new = [jnp.maximum(b_last[h] + m_prev[h],
                         jnp.max(b_last[h] - b_row[h] + i_row[h], axis=1, keepdims=True)) for h in hs]
    ws_col = [jnp.exp(b_last[h] - b_col[h] + i_col[h] - m_new[h]) for h in hs]
    decay = [jnp.exp(b_last[h] + m_prev[h] - m_new[h]) for h in hs]
    wv = [(ws_col[h] * v[h].astype(F32)).astype(BF16) for h in hs]
    kv = [_dot_tn(k[h], wv[h]) for h in hs]
    for h in hs:
        c_ref[h] = decay[h] * c_prev[h] + kv[h]
        n_new = decay[h] * n_prev[h] + jnp.sum(ws_col[h] * k[h].astype(F32), axis=0, keepdims=True)
        n_ref[h] = jnp.broadcast_to(n_new, n_ref.shape[1:])
        m_ref[h] = jnp.broadcast_to(m_new[h], m_ref.shape[1:])
    for u, (bi, h) in enumerate(units):
        og = o_ref[bi, :, h * ML_DV:(h + 1) * ML_DV].astype(F32)
        gain = gain_ref[0:1, h * ML_DV:(h + 1) * ML_DV]
        y = _rms(hc[u], gain) * _sigmoid(og)
        out_ref[bi, :, h * ML_DV:(h + 1) * ML_DV] = y.astype(out_ref.dtype)


def _mlstm(p, gcol, grow, head_gain, bsz, s, *, bb):
    nc = s // CHUNK
    qk_w = ML_HEADS * ML_DQK
    v_w = ML_HEADS * ML_DV
    ng = 2 * ML_HEADS
    return pl.pallas_call(
        _mlstm_kernel,
        grid=(bsz // bb, nc),
        in_specs=[
            pl.BlockSpec((bb, CHUNK, qk_w), lambda b, c: (b, c, 0)),
            pl.BlockSpec((bb, CHUNK, qk_w), lambda b, c: (b, c, 1)),
            pl.BlockSpec((bb, CHUNK, v_w), lambda b, c: (b, c, 1)),
            pl.BlockSpec((bb, CHUNK, v_w), lambda b, c: (b, c, 2)),
            pl.BlockSpec((bb, CHUNK, ng), lambda b, c: (b, c, 0)),
            pl.BlockSpec((bb, 1, ng, CHUNK), lambda b, c: (b, c, 0, 0)),
            pl.BlockSpec((1, v_w), lambda b, c: (0, 0)),
        ],
        out_specs=pl.BlockSpec((bb, CHUNK, v_w), lambda b, c: (b, c, 0)),
        out_shape=jax.ShapeDtypeStruct((bsz, s, v_w), BF16),
        scratch_shapes=[
            pltpu.VMEM((bb * ML_HEADS, ML_DQK, ML_DV), F32),
            pltpu.VMEM((bb * ML_HEADS, 8, ML_DQK), F32),
            pltpu.VMEM((bb * ML_HEADS, 8, 128), F32),
        ],
        compiler_params=pltpu.CompilerParams(dimension_semantics=("parallel", "arbitrary")),
        name="mlstm",
    )(p, p, p, p, gcol, grow, head_gain.reshape(1, v_w).astype(F32))


def _unit_lower_inverse_all(a_list):
    L = a_list[0].shape[0]
    eye = (lax.broadcasted_iota(jnp.int32, (L, L), 0) == lax.broadcasted_iota(jnp.int32, (L, L), 1)).astype(F32)
    xs = [-a for a in a_list]
    ps = [eye + x for x in xs]
    steps = max(1, (L - 1).bit_length()) - 1
    for _ in range(steps):
        x16 = [x.astype(BF16) for x in xs]
        xs = [_dot(xb, xb) for xb in x16]
        x16 = [x.astype(BF16) for x in xs]
        ps = [p + _dot(p.astype(BF16), xb) for p, xb in zip(ps, x16)]
    return ps


def _gdn_kernel(q_ref, k_ref, v_ref, z_ref, gcol_ref, grow_ref, cw_ref, gain_ref, out_ref,
                s_ref, halo_ref):
    L = CHUNK
    H = GD_HEADS
    kw = H * GD_DK

    @pl.when(pl.program_id(1) == 0)
    def _():
        s_ref[...] = jnp.zeros_like(s_ref)
        halo_ref[...] = jnp.zeros_like(halo_ref)

    row = lax.broadcasted_iota(jnp.int32, (L, L), 0)
    col = lax.broadcasted_iota(jnp.int32, (L, L), 1)
    lower = col <= row
    upper = row <= col
    strict = col < row
    halo_rows = halo_ref.shape[2]
    nsh = GD_CONV_K - 1
    rr = lax.broadcasted_iota(jnp.int32, (nsh * L, halo_rows + L), 0)
    cc = lax.broadcasted_iota(jnp.int32, (nsh * L, halo_rows + L), 1)
    shifts = jnp.where(cc == halo_rows + (rr % L) - (rr // L + 1), 1.0, 0.0).astype(BF16)

    def conv_silu(x_ref, part, bi):
        xb = x_ref[bi]
        width = xb.shape[1]
        ext = jnp.concatenate([halo_ref[bi, part, :, 0:width], xb], axis=0)
        r = _dot(shifts, ext)
        cw = cw_ref[:, part * kw:part * kw + width]
        acc = xb.astype(F32) * cw[GD_CONV_K - 1:GD_CONV_K, :]
        for sft in range(1, GD_CONV_K):
            acc = acc + r[(sft - 1) * L:sft * L, :] * cw[GD_CONV_K - 1 - sft:GD_CONV_K - sft, :]
        halo_ref[bi, part, :, 0:width] = xb[L - halo_rows:L, :]
        return _silu(acc)

    bb = q_ref.shape[0]
    qa = [conv_silu(q_ref, 0, bi) for bi in range(bb)]
    ka = [conv_silu(k_ref, 1, bi) for bi in range(bb)]
    va = [conv_silu(v_ref, 2, bi) for bi in range(bb)]

    units = [(bi, h) for bi in range(bb) for h in range(H)]
    hs = range(len(units))
    beta = [gcol_ref[bi, :, h:h + 1] for bi, h in units]
    gc = [_cumsum_col_row(gcol_ref[bi, :, H + h:H + h + 1], grow_ref[bi, 0, H + h:H + h + 1, :], lower, upper)
          for bi, h in units]
    gc_col = [c for c, _ in gc]
    gamma = [jnp.where(lower, jnp.exp(jnp.where(lower, c - r, 0.0)), 0.0) for c, r in gc]
    eg = [jnp.exp(c) for c in gc_col]
    g_last = [c[L - 1:L, :] for c in gc_col]

    q = [qa[bi][:, h * GD_DK:(h + 1) * GD_DK] for bi, h in units]
    k = [ka[bi][:, h * GD_DK:(h + 1) * GD_DK] for bi, h in units]
    v = [va[bi][:, h * GD_DV:(h + 1) * GD_DV] for bi, h in units]
    q = [x * lax.rsqrt(jnp.sum(x * x, axis=1, keepdims=True) + EPS) * (GD_DK ** -0.5) for x in q]
    k = [x * lax.rsqrt(jnp.sum(x * x, axis=1, keepdims=True) + EPS) for x in k]
    kb = [k[h] * beta[h] for h in hs]
    k16 = [x.astype(BF16) for x in k]

    kq = [_dot_nt(jnp.concatenate([kb[h], q[h]], axis=0).astype(BF16), k16[h]) for h in hs]
    a = [jnp.where(strict, kq[h][0:L] * gamma[h], 0.0) for h in hs]
    attn = [(kq[h][L:2 * L] * gamma[h]).astype(BF16) for h in hs]
    t = [x.astype(BF16) for x in _unit_lower_inverse_all(a)]
    uw = [_dot(t[h], jnp.concatenate([v[h] * beta[h], kb[h] * eg[h]], axis=1).astype(BF16)) for h in hs]

    s_prev = [s_ref[h] for h in hs]
    s16 = [x.astype(BF16) for x in s_prev]
    ws = [_dot(jnp.concatenate([uw[h][:, GD_DV:], q[h] * eg[h]], axis=0).astype(BF16), s16[h]) for h in hs]
    vn16 = [(uw[h][:, 0:GD_DV] - ws[h][0:L]).astype(BF16) for h in hs]
    o = [ws[h][L:2 * L] + _dot(attn[h], vn16[h]) for h in hs]
    kdec = [(k[h] * jnp.exp(g_last[h] - gc_col[h])).astype(BF16) for h in hs]
    for h in hs:
        s_ref[h] = jnp.exp(g_last[h]) * s_prev[h] + _dot_tn(kdec[h], vn16[h])
    for u, (bi, h) in enumerate(units):
        z = z_ref[bi, :, h * GD_DV:(h + 1) * GD_DV].astype(F32)
        y = _rms(o[u], gain_ref[...]) * _silu(z)
        out_ref[bi, :, h * GD_DV:(h + 1) * GD_DV] = y.astype(out_ref.dtype)


def _gdn(p, gcol, grow, conv_w, out_gain, bsz, s, *, bb):
    nc = s // CHUNK
    kw = GD_HEADS * GD_DK
    vw = GD_HEADS * GD_DV
    ng = 2 * GD_HEADS
    cc = conv_w.shape[1]
    return pl.pallas_call(
        _gdn_kernel,
        grid=(bsz // bb, nc),
        in_specs=[
            pl.BlockSpec((bb, CHUNK, kw), lambda b, c: (b, c, 0)),
            pl.BlockSpec((bb, CHUNK, kw), lambda b, c: (b, c, 1)),
            pl.BlockSpec((bb, CHUNK, vw), lambda b, c: (b, c, 2)),
            pl.BlockSpec((bb, CHUNK, vw), lambda b, c: (b, c, 3)),
            pl.BlockSpec((bb, CHUNK, ng), lambda b, c: (b, c, 0)),
            pl.BlockSpec((bb, 1, ng, CHUNK), lambda b, c: (b, c, 0, 0)),
            pl.BlockSpec((GD_CONV_K, cc), lambda b, c: (0, 0)),
            pl.BlockSpec((1, GD_DV), lambda b, c: (0, 0)),
        ],
        out_specs=pl.BlockSpec((bb, CHUNK, vw), lambda b, c: (b, c, 0)),
        out_shape=jax.ShapeDtypeStruct((bsz, s, vw), BF16),
        scratch_shapes=[
            pltpu.VMEM((bb * GD_HEADS, GD_DK, GD_DV), F32),
            pltpu.VMEM((bb, 3, 16, kw), BF16),
        ],
        compiler_params=pltpu.CompilerParams(dimension_semantics=("parallel", "arbitrary")),
        name="gdn",
    )(p, p, p, p, gcol, grow, conv_w.astype(F32), out_gain.reshape(1, GD_DV).astype(F32))


def _ffn_kernel(a_ref, wo_ref, res_ref, g_ref, wg_ref, wu_ref, wd_ref, out_ref, h_ref, acc_ref):
    j = pl.program_id(1)

    @pl.when(j == 0)
    def _():
        x1 = res_ref[...] + _dot(a_ref[...], wo_ref[...])
        h_ref[...] = _rms(x1, g_ref[...]).astype(BF16)
        acc_ref[...] = x1

    h = h_ref[...]
    a = _silu(_dot(h, wg_ref[...])) * _dot(h, wu_ref[...])
    acc_ref[...] += _dot(a.astype(BF16), wd_ref[...])

    @pl.when(j == pl.num_programs(1) - 1)
    def _():
        out_ref[...] = acc_ref[...]


def _outproj_ffn(a, w_out, res, gain, w_gate, w_up, w_down, *, tm, tf):
    n, d = res.shape
    k = a.shape[1]
    f = w_gate.shape[1]
    return pl.pallas_call(
        _ffn_kernel,
        grid=(n // tm, f // tf),
        in_specs=[
            pl.BlockSpec((tm, k), lambda i, j: (i, 0)),
            pl.BlockSpec((k, d), lambda i, j: (0, 0)),
            pl.BlockSpec((tm, d), lambda i, j: (i, 0)),
            pl.BlockSpec((1, d), lambda i, j: (0, 0)),
            pl.BlockSpec((d, tf), lambda i, j: (0, j)),
            pl.BlockSpec((d, tf), lambda i, j: (0, j)),
            pl.BlockSpec((tf, d), lambda i, j: (j, 0)),
        ],
        out_specs=pl.BlockSpec((tm, d), lambda i, j: (i, 0)),
        out_shape=jax.ShapeDtypeStruct((n, d), F32),
        scratch_shapes=[pltpu.VMEM((tm, d), BF16), pltpu.VMEM((tm, d), F32)],
        compiler_params=pltpu.CompilerParams(dimension_semantics=("parallel", "arbitrary")),
        name="ffn",
    )(a, w_out.astype(BF16), res, gain.reshape(1, d), w_gate.astype(BF16), w_up.astype(BF16),
      w_down.astype(BF16))


def _expert_kernel(blk_ref, exp_ref, nvalid_ref, x_ref, wg_ref, wu_ref, wd_ref, out_ref, h_ref, acc_ref):
    del blk_ref, exp_ref
    i = pl.program_id(0)
    j = pl.program_id(1)

    @pl.when(i < nvalid_ref[0])
    def _():
        @pl.when(j == 0)
        def _():
            h_ref[...] = _slabs_to_rows(x_ref, h_ref.shape[0]).astype(BF16)
            acc_ref[...] = jnp.zeros_like(acc_ref)

        h = h_ref[...]
        a = _silu(_dot(h, wg_ref[0])) * _dot(h, wu_ref[0])
        acc_ref[...] += _dot(a.astype(BF16), wd_ref[0])

        @pl.when(j == pl.num_programs(1) - 1)
        def _():
            _rows_to_slabs(out_ref, acc_ref[...])

    @pl.when(i >= nvalid_ref[0])
    def _():
        out_ref[...] = jnp.zeros_like(out_ref)


def _experts(xs, tile_blk, tile_exp, n_valid, w_gate, w_up, w_down, *, tm, tf):
    d = w_gate.shape[1]
    dc = d // LANES
    m = xs.shape[0] // dc
    f = w_gate.shape[2]
    nt = m // tm
    nf = f // tf

    def col(i, j, nv):
        return jnp.where(i < nv[0], j, nf - 1)

    grid_spec = pltpu.PrefetchScalarGridSpec(
        num_scalar_prefetch=3,
        grid=(nt, nf),
        in_specs=[
            pl.BlockSpec((tm * dc, LANES), lambda i, j, blk, ex, nv: (blk[i], 0)),
            pl.BlockSpec((1, d, tf), lambda i, j, blk, ex, nv: (ex[i], 0, col(i, j, nv))),
            pl.BlockSpec((1, d, tf), lambda i, j, blk, ex, nv: (ex[i], 0, col(i, j, nv))),
            pl.BlockSpec((1, tf, d), lambda i, j, blk, ex, nv: (ex[i], col(i, j, nv), 0)),
        ],
        out_specs=pl.BlockSpec((tm * dc, LANES), lambda i, j, blk, ex, nv: (i, 0)),
        scratch_shapes=[pltpu.VMEM((tm, d), BF16), pltpu.VMEM((tm, d), F32)],
    )
    return pl.pallas_call(
        _expert_kernel,
        grid_spec=grid_spec,
        out_shape=jax.ShapeDtypeStruct((m * dc, LANES), F32),
        compiler_params=pltpu.CompilerParams(dimension_semantics=("arbitrary", "arbitrary")),
        name="experts",
    )(tile_blk, tile_exp, n_valid, xs, w_gate.astype(BF16), w_up.astype(BF16), w_down.astype(BF16))


def _router_kernel(a_ref, wo_ref, res_ref, g_ref, rt_ref,
                   x_ref, h_ref, idx_ref, gate_ref, rank_ref, cnt_ref, carry_ref):
    tm = res_ref.shape[0]
    ne = rt_ref.shape[0]

    @pl.when(pl.program_id(0) == 0)
    def _():
        carry_ref[...] = jnp.zeros_like(carry_ref)

    x = res_ref[...] + _dot(a_ref[...], wo_ref[...])
    x_ref[...] = x
    h = _rms(x, g_ref[...])
    _rows_to_slabs(h_ref, h)
    logits = _dot_nt(rt_ref[...], h.astype(BF16))
    eidx = lax.broadcasted_iota(jnp.int32, logits.shape, 0)
    m1 = jnp.max(logits, axis=0, keepdims=True)
    i1 = jnp.min(jnp.where(logits == m1, eidx, ne), axis=0, keepdims=True)
    rest = jnp.where(eidx == i1, -jnp.inf, logits)
    m2 = jnp.max(rest, axis=0, keepdims=True)
    i2 = jnp.min(jnp.where(rest == m2, eidx, ne), axis=0, keepdims=True)
    e2 = jnp.exp(m2 - m1)
    den = 1.0 + e2
    idx_ref[...] = jnp.concatenate([i1, i2], axis=0)
    gate_ref[...] = jnp.concatenate([1.0 / den, e2 / den], axis=0)

    sel1 = eidx == i1
    sel2 = eidx == i2
    member = jnp.where(sel1 | sel2, 1.0, 0.0)
    before = (lax.broadcasted_iota(jnp.int32, (tm, tm), 0) < lax.broadcasted_iota(jnp.int32, (tm, tm), 1))
    excl = _dot(member.astype(BF16), jnp.where(before, 1.0, 0.0).astype(BF16))
    excl = excl + carry_ref[:, 0:1]
    r1 = jnp.sum(jnp.where(sel1, excl, 0.0), axis=0, keepdims=True)
    r2 = jnp.sum(jnp.where(sel2, excl, 0.0), axis=0, keepdims=True)
    rank_ref[...] = jnp.concatenate([r1, r2], axis=0).astype(jnp.int32)
    total = carry_ref[:, 0:1] + jnp.sum(member, axis=1, keepdims=True)
    carry_ref[...] = jnp.broadcast_to(total, carry_ref.shape)
    cnt_ref[...] = jnp.broadcast_to(total, cnt_ref.shape).astype(jnp.int32)


def _outproj_router(a, w_out, res, gain, router, *, tm):
    n, d = res.shape
    k = a.shape[1]
    ne = router.shape[1]
    return pl.pallas_call(
        _router_kernel,
        grid=(n // tm,),
        in_specs=[
            pl.BlockSpec((tm, k), lambda i: (i, 0)),
            pl.BlockSpec((k, d), lambda i: (0, 0)),
            pl.BlockSpec((tm, d), lambda i: (i, 0)),
            pl.BlockSpec((1, d), lambda i: (0, 0)),
            pl.BlockSpec((ne, d), lambda i: (0, 0)),
        ],
        out_specs=[
            pl.BlockSpec((tm, d), lambda i: (i, 0)),
            pl.BlockSpec((tm * (d // LANES), LANES), lambda i: (i, 0)),
            pl.BlockSpec((TOP_K, tm), lambda i: (0, i)),
            pl.BlockSpec((TOP_K, tm), lambda i: (0, i)),
            pl.BlockSpec((TOP_K, tm), lambda i: (0, i)),
            pl.BlockSpec((ne, 128), lambda i: (0, 0)),
        ],
        out_shape=[
            jax.ShapeDtypeStruct((n, d), F32),
            jax.ShapeDtypeStruct((n * (d // LANES), LANES), F32),
            jax.ShapeDtypeStruct((TOP_K, n), jnp.int32),
            jax.ShapeDtypeStruct((TOP_K, n), F32),
            jax.ShapeDtypeStruct((TOP_K, n), jnp.int32),
            jax.ShapeDtypeStruct((ne, 128), jnp.int32),
        ],
        scratch_shapes=[pltpu.VMEM((ne, 128), F32)],
        compiler_params=pltpu.CompilerParams(dimension_semantics=("arbitrary",)),
        name="router",
    )(a, w_out.astype(BF16), res, gain.reshape(1, d), router.T.astype(BF16))


def _scatter_kernel(pad_start_ref, pad_len_ref, pos_ref, h_ref, xs_ref, zero_ref, sem, zsem, *, tm_exp):
    tm = pos_ref.shape[2]
    dc = h_ref.shape[0] // tm
    ne = pad_start_ref.shape[0] - 1
    bits = [1 << b for b in reversed(range((tm_exp - 1).bit_length()))]

    def zero_copy(first_row, nrows):
        dst = pl.multiple_of(first_row * dc, dc)
        return pltpu.make_async_copy(zero_ref.at[pl.ds(0, nrows * dc)], xs_ref.at[pl.ds(dst, nrows * dc)], zsem)

    def for_each_zero_copy(act):
        for e in range(ne):
            for bit in bits:
                @pl.when((pad_len_ref[e] & bit) != 0)
                def _(e=e, bit=bit):
                    higher = pad_len_ref[e] & ~(2 * bit - 1)
                    act(zero_copy(pad_start_ref[e] + higher, bit))
        for t in range(ne):
            @pl.when(t < pad_len_ref[ne])
            def _(t=t):
                act(zero_copy(pad_start_ref[ne] + t * tm_exp, tm_exp))

    def row_copy(r, slot):
        src = pl.multiple_of(r * dc, dc)
        dst = pl.multiple_of(pos_ref[0, slot, r] * dc, dc)
        return pltpu.make_async_copy(h_ref.at[pl.ds(src, dc)], xs_ref.at[pl.ds(dst, dc)], sem)

    def start(r, carry):
        for slot in range(TOP_K):
            row_copy(r, slot).start(priority=slot)
        return carry

    def wait(r, carry):
        for slot in range(TOP_K):
            row_copy(r, slot).wait()
        return carry

    @pl.when(pl.program_id(0) == 0)
    def _():
        zero_ref[...] = jnp.zeros_like(zero_ref)
        for_each_zero_copy(lambda c: c.start())

    lax.fori_loop(0, tm, start, 0, unroll=8)

    @pl.when(pl.program_id(0) == 0)
    def _():
        for_each_zero_copy(lambda c: c.wait())

    lax.fori_loop(0, tm, wait, 0, unroll=8)


def _scatter_rows(h, pos_tiles, pad_start, pad_len, m_rows, *, tm, tm_exp):
    n = pos_tiles.shape[0] * tm
    dc = h.shape[0] // n
    grid_spec = pltpu.PrefetchScalarGridSpec(
        num_scalar_prefetch=2,
        grid=(n // tm,),
        in_specs=[
            pl.BlockSpec((1, TOP_K, tm), lambda i, ps, pn: (i, 0, 0), memory_space=pltpu.SMEM),
            pl.BlockSpec((tm * dc, LANES), lambda i, ps, pn: (i, 0)),
        ],
        out_specs=pl.BlockSpec(memory_space=pl.ANY),
        scratch_shapes=[pltpu.VMEM((tm_exp * dc, LANES), F32), pltpu.SemaphoreType.DMA(()),
                        pltpu.SemaphoreType.DMA(())],
    )
    return pl.pallas_call(
        functools.partial(_scatter_kernel, tm_exp=tm_exp),
        grid_spec=grid_spec,
        out_shape=jax.ShapeDtypeStruct((m_rows * dc, LANES), F32),
        compiler_params=pltpu.CompilerParams(dimension_semantics=("arbitrary",), has_side_effects=True),
        name="scatter_rows",
    )(pad_start, pad_len, pos_tiles, h)


def _combine_kernel(pos_ref, pos_next_ref, x_ref, gate_ref, g_ref, ys_ref, out_ref, rows_ref, sems):
    tm = x_ref.shape[0]
    i = pl.program_id(0)
    n_steps = pl.num_programs(0)
    buf = i % 2

    dc = rows_ref.shape[2] // tm

    def row_copy(p_ref, b, r, slot):
        src = pl.multiple_of(p_ref[0, slot, r] * dc, dc)
        dst = pl.multiple_of(r * dc, dc)
        return pltpu.make_async_copy(ys_ref.at[pl.ds(src, dc)], rows_ref.at[b, slot, pl.ds(dst, dc)], sems.at[b])

    def start_tile(p_ref, b):
        def body(r, carry):
            for slot in range(TOP_K):
                row_copy(p_ref, b, r, slot).start(priority=slot)
            return carry
        lax.fori_loop(0, tm, body, 0, unroll=8)

    @pl.when(i == 0)
    def _():
        start_tile(pos_ref, 0)

    @pl.when(i + 1 < n_steps)
    def _():
        start_tile(pos_next_ref, 1 - buf)

    def wait(r, carry):
        for slot in range(TOP_K):
            row_copy(pos_ref, buf, r, slot).wait()
        return carry

    lax.fori_loop(0, tm, wait, 0, unroll=8)
    y = (x_ref[...] + gate_ref[:, 0:1] * _slabs_to_rows(rows_ref.at[buf, 0], tm)
         + gate_ref[:, 1:2] * _slabs_to_rows(rows_ref.at[buf, 1], tm))
    out_ref[...] = _rms(y, g_ref[...])


def _combine(x2, pos_tiles, gate_col, gain, ys, *, tm):
    n, d = x2.shape
    dc = d // LANES
    nt = n // tm
    return pl.pallas_call(
        _combine_kernel,
        grid=(nt,),
        in_specs=[
            pl.BlockSpec((1, TOP_K, tm), lambda i: (i, 0, 0), memory_space=pltpu.SMEM),
            pl.BlockSpec((1, TOP_K, tm), lambda i: (jnp.minimum(i + 1, nt - 1), 0, 0), memory_space=pltpu.SMEM),
            pl.BlockSpec((tm, d), lambda i: (i, 0)),
            pl.BlockSpec((tm, TOP_K), lambda i: (i, 0)),
            pl.BlockSpec((1, d), lambda i: (0, 0)),
            pl.BlockSpec(memory_space=pl.ANY),
        ],
        out_specs=pl.BlockSpec((tm, d), lambda i: (i, 0)),
        out_shape=jax.ShapeDtypeStruct((n, d), F32),
        scratch_shapes=[pltpu.VMEM((2, TOP_K, tm * dc, LANES), F32), pltpu.SemaphoreType.DMA((2,))],
        compiler_params=pltpu.CompilerParams(dimension_semantics=("arbitrary",)),
        name="combine",
    )(pos_tiles, pos_tiles, x2, gate_col, gain.reshape(1, d), ys)


def _moe_and_final_norm(a, w_out, res, norm_gain, router, w_gate, w_up, w_down, final_gain,
                        *, tm_route, tm_exp, tf):
    n, d = res.shape
    ne = router.shape[1]
    x2, h, idx_t, gate_t, rank_t, counts = _outproj_router(a, w_out, res, norm_gain, router, tm=tm_route)

    cnt = counts[:, 0]
    padded = ((cnt + tm_exp - 1) // tm_exp) * tm_exp
    ends = jnp.cumsum(padded)
    starts = ends - padded
    pos = rank_t
    for e in range(ne):
        pos = pos + jnp.where(idx_t == e, starts[e], 0)
    pos_tiles = pos.reshape(TOP_K, n // tm_route, tm_route).transpose(1, 0, 2)
    m_rows = TOP_K * n + ne * tm_exp
    nt = m_rows // tm_exp
    n_valid = (ends[-1] // tm_exp).astype(jnp.int32)
    tile_blk = jnp.minimum(jnp.arange(nt, dtype=jnp.int32), n_valid - 1)
    tile_exp = jnp.minimum(
        jnp.sum(ends[None, :] <= (tile_blk * tm_exp)[:, None], axis=1).astype(jnp.int32), ne - 1)

    pad_start = jnp.concatenate([starts + cnt, ends[-1:]]).astype(jnp.int32)
    pad_len = jnp.concatenate([padded - cnt, (nt - n_valid).reshape(1)]).astype(jnp.int32)
    xs = _scatter_rows(h, pos_tiles, pad_start, pad_len, m_rows, tm=tm_route, tm_exp=tm_exp)
    ys = _experts(xs, tile_blk, tile_exp, n_valid.reshape(1), w_gate, w_up, w_down, tm=tm_exp, tf=tf)
    return _combine(x2, pos_tiles, gate_t.T, final_gain, ys, tm=tm_route)


def _trunk(x, l0_norm_mix, l0_ml_w_in, l0_ml_b_if, l0_ml_head_gain, l0_ml_w_out,
           l0_norm_ffn, l0_ffn_w_gate, l0_ffn_w_up, l0_ffn_w_down,
           l1_norm_mix, l1_gdn_w_in, l1_gdn_conv, l1_gdn_a_log, l1_gdn_dt_bias,
           l1_gdn_out_gain, l1_gdn_w_out, l1_norm_ffn, l1_moe_router,
           l1_moe_w_gate, l1_moe_w_up, l1_moe_w_down, final_norm, *, tiles):
    bsz, s, d = x.shape
    n = bsz * s
    nc = s // CHUNK
    x2 = x.reshape(n, d)

    qk_w = ML_HEADS * ML_DQK
    v_w = ML_HEADS * ML_DV
    fm = 2 * qk_w + 2 * v_w
    col_scale = jnp.concatenate([jnp.ones((qk_w,), F32), jnp.full((qk_w,), ML_DQK ** -0.5, F32),
                                 jnp.ones((2 * v_w,), F32)])
    ng = 2 * ML_HEADS
    p, gcol, grow = _inproj(x2, l0_norm_mix, l0_ml_w_in[:, :fm] * col_scale, l0_ml_w_in[:, fm:],
                            l0_ml_b_if, jnp.zeros((ng,), F32), _mlstm_gate_act,
                            tm=tiles["tm_proj"], tn=tiles["tn_proj"])
    grow = grow.reshape(ng, bsz, nc, CHUNK).transpose(1, 2, 0, 3)
    hs = _mlstm(p.reshape(bsz, s, fm), gcol.reshape(bsz, s, ng), grow, l0_ml_head_gain, bsz, s,
                bb=tiles["bb_ml"])

    x2 = _outproj_ffn(hs.reshape(n, v_w), l0_ml_w_out, x2, l0_norm_ffn, l0_ffn_w_gate, l0_ffn_w_up,
                      l0_ffn_w_down, tm=tiles["tm_ffn"], tf=tiles["tf"])

    kw = GD_HEADS * GD_DK
    vw = GD_HEADS * GD_DV
    fm = 2 * kw + 2 * vw
    ng = 2 * GD_HEADS
    zeros_h = jnp.zeros((GD_HEADS,), F32)
    p, gcol, grow = _inproj(x2, l1_norm_mix, l1_gdn_w_in[:, :fm], l1_gdn_w_in[:, fm:],
                            jnp.concatenate([zeros_h, l1_gdn_dt_bias]),
                            jnp.concatenate([zeros_h, l1_gdn_a_log]), _gdn_gate_act,
                            tm=tiles["tm_proj"], tn=tiles["tn_proj"])
    grow = grow.reshape(ng, bsz, nc, CHUNK).transpose(1, 2, 0, 3)
    o = _gdn(p.reshape(bsz, s, fm), gcol.reshape(bsz, s, ng), grow, l1_gdn_conv, l1_gdn_out_gain, bsz, s,
             bb=tiles["bb_gd"])

    out = _moe_and_final_norm(o.reshape(n, vw), l1_gdn_w_out, x2, l1_norm_ffn, l1_moe_router,
                              l1_moe_w_gate, l1_moe_w_up, l1_moe_w_down, final_norm,
                              tm_route=tiles["tm_route"], tm_exp=tiles["tm_exp"], tf=tiles["tf"])
    return out.reshape(bsz, s, d)


def _tiles_for(n, bsz, f):
    cap = lambda t: min(t, n)
    return dict(tm_proj=cap(512), tn_proj=1024, tm_ffn=cap(512), tf=f // 2,
                tm_route=cap(512), tm_exp=cap(512), bb_ml=min(4, bsz), bb_gd=min(4, bsz))


def kernel(x, l0_norm_mix, l0_ml_w_in, l0_ml_b_if, l0_ml_head_gain, l0_ml_w_out, l0_norm_ffn, l0_ffn_w_gate, l0_ffn_w_up, l0_ffn_w_down, l1_norm_mix, l1_gdn_w_in, l1_gdn_conv, l1_gdn_a_log, l1_gdn_dt_bias, l1_gdn_out_gain, l1_gdn_w_out, l1_norm_ffn, l1_moe_router, l1_moe_w_gate, l1_moe_w_up, l1_moe_w_down, final_norm):
    n = x.shape[0] * x.shape[1]
    return _trunk(x, l0_norm_mix, l0_ml_w_in, l0_ml_b_if, l0_ml_head_gain, l0_ml_w_out,
                  l0_norm_ffn, l0_ffn_w_gate, l0_ffn_w_up, l0_ffn_w_down,
                  l1_norm_mix, l1_gdn_w_in, l1_gdn_conv, l1_gdn_a_log, l1_gdn_dt_bias,
                  l1_gdn_out_gain, l1_gdn_w_out, l1_norm_ffn, l1_moe_router,
                  l1_moe_w_gate, l1_moe_w_up, l1_moe_w_down, final_norm, tiles=_tiles_for(n, x.shape[0], l0_ffn_w_gate.shape[1]))
```

```python
import functools

import jax
import jax.numpy as jnp
from jax import lax
from jax.experimental import pallas as pl
from jax.experimental.pallas import tpu as pltpu

EPS = 1e-6
CHUNK = 64

ML_HEADS, ML_DQK, ML_DV = 4, 128, 256
GD_HEADS, GD_DK, GD_DV = 8, 128, 128
GD_CONV_K = 4
N_EXPERTS, TOP_K = 8, 2

F32 = jnp.float32
BF16 = jnp.bfloat16

_NT = (((1,), (1,)), ((), ()))
_TN = (((0,), (0,)), ((), ()))


def _dot(a, b):
    return jnp.dot(a, b, preferred_element_type=F32)


def _dot_nt(a, b):
    return lax.dot_general(a, b, _NT, preferred_element_type=F32)


def _dot_tn(a, b):
    return lax.dot_general(a, b, _TN, preferred_element_type=F32)


def _rms(x, g):
    return x * lax.rsqrt(jnp.mean(x * x, axis=-1, keepdims=True) + EPS) * g


def _softplus(x):
    return jnp.maximum(x, 0.0) + jnp.log(1.0 + jnp.exp(-jnp.abs(x)))


def _sigmoid(x):
    return 1.0 / (1.0 + jnp.exp(-x))


def _silu(x):
    return x * _sigmoid(x)


LANES = 128


def _rows_to_slabs(ref, x):
    rows, d = x.shape
    dc = d // LANES
    for c in range(dc):
        ref[pl.ds(c, rows, stride=dc), :] = x[:, c * LANES:(c + 1) * LANES]


def _slabs_to_rows(ref, rows):
    dc = ref.shape[0] // rows
    return jnp.concatenate([ref[pl.ds(c, rows, stride=dc), :] for c in range(dc)], axis=1)


def _mlstm_gate_act(pre, gidx, p1, p2):
    del p2
    z = pre + p1
    return jnp.where(gidx < ML_HEADS, z, -_softplus(-z))


def _gdn_gate_act(pre, gidx, p1, p2):
    beta = _sigmoid(pre)
    g = -(jnp.exp(p2) * _softplus(pre + p1))
    return jnp.where(gidx < GD_HEADS, beta, g)


def _inproj_kernel(x_ref, g_ref, w_ref, wg_ref, wgt_ref, pc_ref, pr_ref,
                   main_ref, gcol_ref, grow_ref, *, act, tn):
    h = _rms(x_ref[...], g_ref[...]).astype(BF16)
    pre_c = _dot(h, wg_ref[...])
    gi_c = lax.broadcasted_iota(jnp.int32, pre_c.shape, 1)
    gcol_ref[...] = act(pre_c, gi_c, pc_ref[0:1, :], pc_ref[1:2, :])
    pre_r = _dot_nt(wgt_ref[...], h)
    gi_r = lax.broadcasted_iota(jnp.int32, pre_r.shape, 0)
    grow_ref[...] = act(pre_r, gi_r, pr_ref[:, 0:1], pr_ref[:, 1:2])
    for c in range(0, w_ref.shape[1], tn):
        main_ref[:, c:c + tn] = _dot(h, w_ref[:, c:c + tn]).astype(main_ref.dtype)


def _inproj(x2, gain, w_main, w_gate, p1, p2, act, *, tm, tn):
    n, d = x2.shape
    fm = w_main.shape[1]
    ng = w_gate.shape[1]
    pc = jnp.stack([p1, p2]).astype(F32)
    pr = pc.T
    return pl.pallas_call(
        functools.partial(_inproj_kernel, act=act, tn=tn),
        grid=(n // tm,),
        in_specs=[
            pl.BlockSpec((tm, d), lambda i: (i, 0)),
            pl.BlockSpec((1, d), lambda i: (0, 0)),
            pl.BlockSpec((d, fm), lambda i: (0, 0)),
            pl.BlockSpec((d, ng), lambda i: (0, 0)),
            pl.BlockSpec((ng, d), lambda i: (0, 0)),
            pl.BlockSpec((2, ng), lambda i: (0, 0)),
            pl.BlockSpec((ng, 2), lambda i: (0, 0)),
        ],
        out_specs=[
            pl.BlockSpec((tm, fm), lambda i: (i, 0)),
            pl.BlockSpec((tm, ng), lambda i: (i, 0)),
            pl.BlockSpec((ng, tm), lambda i: (0, i)),
        ],
        out_shape=[
            jax.ShapeDtypeStruct((n, fm), BF16),
            jax.ShapeDtypeStruct((n, ng), F32),
            jax.ShapeDtypeStruct((ng, n), F32),
        ],
        compiler_params=pltpu.CompilerParams(dimension_semantics=("parallel",)),
        name="inproj",
    )(x2, gain.reshape(1, d), w_main.astype(BF16), w_gate.astype(BF16), w_gate.T.astype(BF16), pc, pr)


def _cumsum_col_row(v_col, v_row, incl_lower, incl_upper):
    c_col = jnp.sum(jnp.where(incl_lower, v_row, 0.0), axis=1, keepdims=True)
    c_row = jnp.sum(jnp.where(incl_upper, v_col, 0.0), axis=0, keepdims=True)
    return c_col, c_row


def _mlstm_kernel(q_ref, k_ref, v_ref, o_ref, gcol_ref, grow_ref, gain_ref, out_ref,
                  c_ref, n_ref, m_ref):
    L = CHUNK
    H = ML_HEADS

    @pl.when(pl.program_id(1) == 0)
    def _():
        c_ref[...] = jnp.zeros_like(c_ref)
        n_ref[...] = jnp.zeros_like(n_ref)
        m_ref[...] = jnp.zeros_like(m_ref)

    row = lax.broadcasted_iota(jnp.int32, (L, L), 0)
    col = lax.broadcasted_iota(jnp.int32, (L, L), 1)
    lower = col <= row
    upper = row <= col
    units = [(bi, h) for bi in range(q_ref.shape[0]) for h in range(H)]
    hs = range(len(units))
    q = [q_ref[bi, :, h * ML_DQK:(h + 1) * ML_DQK] for bi, h in units]
    k = [k_ref[bi, :, h * ML_DQK:(h + 1) * ML_DQK] for bi, h in units]
    v = [v_ref[bi, :, h * ML_DV:(h + 1) * ML_DV] for bi, h in units]
    i_col = [gcol_ref[bi, :, h:h + 1] for bi, h in units]
    i_row = [grow_ref[bi, 0, h:h + 1, :] for bi, h in units]
    b = [_cumsum_col_row(gcol_ref[bi, :, H + h:H + h + 1], grow_ref[bi, 0, H + h:H + h + 1, :], lower, upper)
         for bi, h in units]
    b_col = [c for c, _ in b]
    b_row = [r for _, r in b]
    m_prev = [m_ref[h, 0:1, 0:1] for h in hs]
    c_prev = [c_ref[h] for h in hs]
    n_prev = [n_ref[h, 0:1, :] for h in hs]

    qk = [_dot_nt(q[h], k[h]) for h in hs]
    qc = [_dot(q[h], c_prev[h].astype(BF16)) for h in hs]
    d = [jnp.where(lower, b_col[h] - b_row[h] + i_row[h], -jnp.inf) for h in hs]
    m_inter = [b_col[h] + m_prev[h] for h in hs]
    m_t = [jnp.maximum(jnp.max(d[h], axis=1, keepdims=True), m_inter[h]) for h in hs]
    sc = [qk[h] * jnp.exp(d[h] - m_t[h]) for h in hs]
    a = [jnp.exp(m_inter[h] - m_t[h]) for h in hs]
    num = [_dot(sc[h].astype(BF16), v[h]) + a[h] * qc[h] for h in hs]
    den = [jnp.sum(sc[h], axis=1, keepdims=True)
           + a[h] * jnp.sum(q[h].astype(F32) * n_prev[h], axis=1, keepdims=True) for h in hs]
    hc = [num[h] / jnp.maximum(jnp.abs(den[h]), jnp.exp(-m_t[h])) for h in hs]

    b_last = [c[L - 1:L, :] for c in b_col]
    m_new = [jnp.maximum(b_last[h] + m_prev[h],
                         jnp.max(b_last[h] - b_row[h] + i_row[h], axis=1, keepdims=True)) for h in hs]
    ws_col = [jnp.exp(b_last[h] - b_col[h] + i_col[h] - m_new[h]) for h in hs]
    decay = [jnp.exp(b_last[h] + m_prev[h] - m_new[h]) for h in hs]
    wv = [(ws_col[h] * v[h].astype(F32)).astype(BF16) for h in hs]
    kv = [_dot_tn(k[h], wv[h]) for h in hs]
    for h in hs:
        c_ref[h] = decay[h] * c_prev[h] + kv[h]
        n_new = decay[h] * n_prev[h] + jnp.sum(ws_col[h] * k[h].astype(F32), axis=0, keepdims=True)
        n_ref[h] = jnp.broadcast_to(n_new, n_ref.shape[1:])
        m_ref[h] = jnp.broadcast_to(m_new[h], m_ref.shape[1:])
    for u, (bi, h) in enumerate(units):
        og = o_ref[bi, :, h * ML_DV:(h + 1) * ML_DV].astype(F32)
        gain = gain_ref[0:1, h * ML_DV:(h + 1) * ML_DV]
        y = _rms(hc[u], gain) * _sigmoid(og)
        out_ref[bi, :, h * ML_DV:(h + 1) * ML_DV] = y.astype(out_ref.dtype)


def _mlstm(p, gcol, grow, head_gain, bsz, s, *, bb):
    nc = s // CHUNK
    qk_w = ML_HEADS * ML_DQK
    v_w = ML_HEADS * ML_DV
    ng = 2 * ML_HEADS
    return pl.pallas_call(
        _mlstm_kernel,
        grid=(bsz // bb, nc),
        in_specs=[
            pl.BlockSpec((bb, CHUNK, qk_w), lambda b, c: (b, c, 0)),
            pl.BlockSpec((bb, CHUNK, qk_w), lambda b, c: (b, c, 1)),
            pl.BlockSpec((bb, CHUNK, v_w), lambda b, c: (b, c, 1)),
            pl.BlockSpec((bb, CHUNK, v_w), lambda b, c: (b, c, 2)),
            pl.BlockSpec((bb, CHUNK, ng), lambda b, c: (b, c, 0)),
            pl.BlockSpec((bb, 1, ng, CHUNK), lambda b, c: (b, c, 0, 0)),
            pl.BlockSpec((1, v_w), lambda b, c: (0, 0)),
        ],
        out_specs=pl.BlockSpec((bb, CHUNK, v_w), lambda b, c: (b, c, 0)),
        out_shape=jax.ShapeDtypeStruct((bsz, s, v_w), BF16),
        scratch_shapes=[
            pltpu.VMEM((bb * ML_HEADS, ML_DQK, ML_DV), F32),
            pltpu.VMEM((bb * ML_HEADS, 8, ML_DQK), F32),
            pltpu.VMEM((bb * ML_HEADS, 8, 128), F32),
        ],
        compiler_params=pltpu.CompilerParams(dimension_semantics=("parallel", "arbitrary")),
        name="mlstm",
    )(p, p, p, p, gcol, grow, head_gain.reshape(1, v_w).astype(F32))


def _pair_block_diag(xp, left):
    zero = jnp.zeros_like(xp)
    return jnp.concatenate([jnp.where(left, xp, zero), jnp.where(left, zero, xp)], axis=0)


def _unit_lower_inverse_pairs(a_list, left, eye):
    L = a_list[0].shape[0]
    xs = [-a for a in a_list]
    ps = [eye + x for x in xs]
    steps = max(1, (L - 1).bit_length()) - 1
    for _ in range(steps):
        x16 = [x.astype(BF16) for x in xs]
        xs = [_dot(xb, _pair_block_diag(xb, left)) for xb in x16]
        xbd = [_pair_block_diag(x.astype(BF16), left) for x in xs]
        ps = [p + _dot(p.astype(BF16), xb) for p, xb in zip(ps, xbd)]
    return ps


def _gdn_kernel(q_ref, k_ref, v_ref, z_ref, gcol_ref, grow_ref, cw_ref, gain_ref, out_ref,
                s_ref, halo_ref):
    L = CHUNK
    H = GD_HEADS
    kw = H * GD_DK

    @pl.when(pl.program_id(1) == 0)
    def _():
        s_ref[...] = jnp.zeros_like(s_ref)
        halo_ref[...] = jnp.zeros_like(halo_ref)

    halo_rows = halo_ref.shape[2]
    nsh = GD_CONV_K - 1
    rr = lax.broadcasted_iota(jnp.int32, (nsh * L, halo_rows + L), 0)
    cc = lax.broadcasted_iota(jnp.int32, (nsh * L, halo_rows + L), 1)
    shifts = jnp.where(cc == halo_rows + (rr % L) - (rr // L + 1), 1.0, 0.0).astype(BF16)

    def conv_silu(x_ref, part, bi):
        xb = x_ref[bi]
        width = xb.shape[1]
        ext = jnp.concatenate([halo_ref[bi, part, :, 0:width], xb], axis=0)
        r = _dot(shifts, ext)
        cw = cw_ref[:, part * kw:part * kw + width]
        acc = xb.astype(F32) * cw[GD_CONV_K - 1:GD_CONV_K, :]
        for sft in range(1, GD_CONV_K):
            acc = acc + r[(sft - 1) * L:sft * L, :] * cw[GD_CONV_K - 1 - sft:GD_CONV_K - sft, :]
        halo_ref[bi, part, :, 0:width] = xb[L - halo_rows:L, :]
        return _silu(acc)

    bb = q_ref.shape[0]
    qa = [conv_silu(q_ref, 0, bi) for bi in range(bb)]
    ka = [conv_silu(k_ref, 1, bi) for bi in range(bb)]
    va = [conv_silu(v_ref, 2, bi) for bi in range(bb)]

    lane = lax.broadcasted_iota(jnp.int32, (L, 2 * L), 1)
    rowp = lax.broadcasted_iota(jnp.int32, (L, 2 * L), 0)
    left = lane < L
    colp = jnp.where(left, lane, lane - L)
    lower_p = colp <= rowp
    upper_p = rowp <= colp
    strict_p = colp < rowp
    eye_p = jnp.where(colp == rowp, 1.0, 0.0)
    zeros_k = jnp.zeros((L, GD_DK), F32)
    zeros_s = jnp.zeros((GD_DK, GD_DV), F32)

    def side_by_side(x0, x1):
        return jnp.concatenate([jnp.concatenate([x0, zeros_k], axis=1),
                                jnp.concatenate([zeros_k, x1], axis=1)], axis=0)

    units = [(bi, hp) for bi in range(bb) for hp in range(H // 2)]
    us = range(len(units))
    heads = [(bi, 2 * hp + j) for bi, hp in units for j in range(2)]
    beta = [gcol_ref[bi, :, h:h + 1] for bi, h in heads]
    g_col = [gcol_ref[bi, :, H + h:H + h + 1] for bi, h in heads]
    g_row_p = [grow_ref[bi, 0, hp:hp + 1, :] for bi, hp in units]
    g_col_p = [jnp.where(left, g_col[2 * u], g_col[2 * u + 1]) for u in us]
    gc_row_p = [jnp.sum(jnp.where(upper_p, g_col_p[u], 0.0), axis=0, keepdims=True) for u in us]
    part = [jnp.where(lower_p, g_row_p[u], 0.0) for u in us]
    gc_col = [jnp.sum(jnp.where(left if j == 0 else ~left, part[u], 0.0), axis=1, keepdims=True)
              for u in us for j in range(2)]
    gc_col_p = [jnp.where(left, gc_col[2 * u], gc_col[2 * u + 1]) for u in us]
    gamma_p = [jnp.where(lower_p, jnp.exp(jnp.where(lower_p, gc_col_p[u] - gc_row_p[u], 0.0)), 0.0) for u in us]
    eg = [jnp.exp(c) for c in gc_col]
    g_last = [c[L - 1:L, :] for c in gc_col]

    q = [qa[bi][:, h * GD_DK:(h + 1) * GD_DK] for bi, h in heads]
    k = [ka[bi][:, h * GD_DK:(h + 1) * GD_DK] for bi, h in heads]
    v = [va[bi][:, h * GD_DV:(h + 1) * GD_DV] for bi, h in heads]
    q = [x * lax.rsqrt(jnp.sum(x * x, axis=1, keepdims=True) + EPS) * (GD_DK ** -0.5) for x in q]
    k = [x * lax.rsqrt(jnp.sum(x * x, axis=1, keepdims=True) + EPS) for x in k]
    kb = [k[h] * beta[h] for h in range(len(heads))]

    kq = [_dot_nt(jnp.concatenate([jnp.concatenate([kb[2 * u], q[2 * u]], axis=0),
                                   jnp.concatenate([kb[2 * u + 1], q[2 * u + 1]], axis=0)], axis=1).astype(BF16),
                  side_by_side(k[2 * u], k[2 * u + 1]).astype(BF16)) for u in us]
    a_p = [jnp.where(strict_p, kq[u][0:L] * gamma_p[u], 0.0) for u in us]
    attn_p = [(kq[u][L:2 * L] * gamma_p[u]).astype(BF16) for u in us]
    t_p = [x.astype(BF16) for x in _unit_lower_inverse_pairs(a_p, left, eye_p)]
    uw = [_dot(t_p[u], jnp.concatenate(
        [jnp.concatenate([v[2 * u] * beta[2 * u], kb[2 * u] * eg[2 * u], zeros_k, zeros_k], axis=1),
         jnp.concatenate([zeros_k, zeros_k, v[2 * u + 1] * beta[2 * u + 1], kb[2 * u + 1] * eg[2 * u + 1]], axis=1)],
        axis=0).astype(BF16)) for u in us]

    s_prev = [s_ref[h] for h in range(len(heads))]
    s_bd = [jnp.concatenate([jnp.concatenate([s_prev[2 * u], zeros_s], axis=1),
                             jnp.concatenate([zeros_s, s_prev[2 * u + 1]], axis=1)], axis=0).astype(BF16) for u in us]
    w_cols = (GD_DV, 2 * GD_DV + GD_DK)
    ws = [_dot(jnp.concatenate(
        [jnp.concatenate([uw[u][:, w_cols[0]:w_cols[0] + GD_DK], uw[u][:, w_cols[1]:w_cols[1] + GD_DK]], axis=1),
         jnp.concatenate([q[2 * u] * eg[2 * u], q[2 * u + 1] * eg[2 * u + 1]], axis=1)], axis=0).astype(BF16),
        s_bd[u]) for u in us]
    u_cols = (0, GD_DV + GD_DK)
    vn = [uw[u][:, u_cols[j]:u_cols[j] + GD_DV] - ws[u][0:L, j * GD_DV:(j + 1) * GD_DV]
          for u in us for j in range(2)]
    o_p = [ws[u][L:2 * L] + _dot(attn_p[u], side_by_side(vn[2 * u], vn[2 * u + 1]).astype(BF16)) for u in us]
    kdec = [(k[h] * jnp.exp(g_last[h] - gc_col[h])).astype(BF16) for h in range(len(heads))]
    for h in range(len(heads)):
        s_ref[h] = jnp.exp(g_last[h]) * s_prev[h] + _dot_tn(kdec[h], vn[h].astype(BF16))
    for i, (bi, h) in enumerate(heads):
        z = z_ref[bi, :, h * GD_DV:(h + 1) * GD_DV].astype(F32)
        o = o_p[i // 2][:, (i % 2) * GD_DV:(i % 2 + 1) * GD_DV]
        y = _rms(o, gain_ref[...]) * _silu(z)
        out_ref[bi, :, h * GD_DV:(h + 1) * GD_DV] = y.astype(out_ref.dtype)


def _gdn_pair_rows(grow, bsz, nc):
    hp = GD_HEADS // 2
    g = grow[GD_HEADS:].reshape(hp, 2, bsz, nc, CHUNK)
    return g.transpose(2, 3, 0, 1, 4).reshape(bsz, nc, hp, 2 * CHUNK)


def _gdn(p, gcol, grow, conv_w, out_gain, bsz, s, *, bb):
    nc = s // CHUNK
    kw = GD_HEADS * GD_DK
    vw = GD_HEADS * GD_DV
    ng = 2 * GD_HEADS
    cc = conv_w.shape[1]
    hp = GD_HEADS // 2
    return pl.pallas_call(
        _gdn_kernel,
        grid=(bsz // bb, nc),
        in_specs=[
            pl.BlockSpec((bb, CHUNK, kw), lambda b, c: (b, c, 0)),
            pl.BlockSpec((bb, CHUNK, kw), lambda b, c: (b, c, 1)),
            pl.BlockSpec((bb, CHUNK, vw), lambda b, c: (b, c, 2)),
            pl.BlockSpec((bb, CHUNK, vw), lambda b, c: (b, c, 3)),
            pl.BlockSpec((bb, CHUNK, ng), lambda b, c: (b, c, 0)),
            pl.BlockSpec((bb, 1, hp, 2 * CHUNK), lambda b, c: (b, c, 0, 0)),
            pl.BlockSpec((GD_CONV_K, cc), lambda b, c: (0, 0)),
            pl.BlockSpec((1, GD_DV), lambda b, c: (0, 0)),
        ],
        out_specs=pl.BlockSpec((bb, CHUNK, vw), lambda b, c: (b, c, 0)),
        out_shape=jax.ShapeDtypeStruct((bsz, s, vw), BF16),
        scratch_shapes=[
            pltpu.VMEM((bb * GD_HEADS, GD_DK, GD_DV), F32),
            pltpu.VMEM((bb, 3, 16, kw), BF16),
        ],
        compiler_params=pltpu.CompilerParams(dimension_semantics=("parallel", "arbitrary")),
        name="gdn",
    )(p, p, p, p, gcol, grow, conv_w.astype(F32), out_gain.reshape(1, GD_DV).astype(F32))


def _ffn_kernel(a_ref, wo_ref, res_ref, g_ref, wg_ref, wu_ref, wd_ref, out_ref, *, tf):
    x1 = res_ref[...] + _dot(a_ref[...], wo_ref[...])
    h = _rms(x1, g_ref[...]).astype(BF16)
    y = x1
    for c in range(0, wg_ref.shape[1], tf):
        a = _silu(_dot(h, wg_ref[:, c:c + tf])) * _dot(h, wu_ref[:, c:c + tf])
        y = y + _dot(a.astype(BF16), wd_ref[c:c + tf, :])
    out_ref[...] = y


def _outproj_ffn(a, w_out, res, gain, w_gate, w_up, w_down, *, tm, tf):
    n, d = res.shape
    k = a.shape[1]
    f = w_gate.shape[1]
    once = dict(pipeline_mode=pl.Buffered(1))
    return pl.pallas_call(
        functools.partial(_ffn_kernel, tf=tf),
        grid=(n // tm,),
        in_specs=[
            pl.BlockSpec((tm, k), lambda i: (i, 0)),
            pl.BlockSpec((k, d), lambda i: (0, 0), **once),
            pl.BlockSpec((tm, d), lambda i: (i, 0)),
            pl.BlockSpec((1, d), lambda i: (0, 0)),
            pl.BlockSpec((d, f), lambda i: (0, 0), **once),
            pl.BlockSpec((d, f), lambda i: (0, 0), **once),
            pl.BlockSpec((f, d), lambda i: (0, 0), **once),
        ],
        out_specs=pl.BlockSpec((tm, d), lambda i: (i, 0)),
        out_shape=jax.ShapeDtypeStruct((n, d), F32),
        compiler_params=pltpu.CompilerParams(dimension_semantics=("parallel",)),
        name="ffn",
    )(a, w_out.astype(BF16), res, gain.reshape(1, d), w_gate.astype(BF16), w_up.astype(BF16),
      w_down.astype(BF16))


def _expert_kernel(blk_ref, exp_ref, nvalid_ref, x_ref, wg_ref, wu_ref, wd_ref, out_ref, *, tm, tf):
    del blk_ref, exp_ref
    i = pl.program_id(0)
    f = wg_ref.shape[2]

    @pl.when(i < nvalid_ref[0])
    def _():
        h = _slabs_to_rows(x_ref, tm).astype(BF16)
        y = None
        for c in range(0, f, tf):
            a = _silu(_dot(h, wg_ref[0, :, c:c + tf])) * _dot(h, wu_ref[0, :, c:c + tf])
            part = _dot(a.astype(BF16), wd_ref[0, c:c + tf, :])
            y = part if y is None else y + part
        _rows_to_slabs(out_ref, y)

    @pl.when(i >= nvalid_ref[0])
    def _():
        out_ref[...] = jnp.zeros_like(out_ref)


def _experts(xs, tile_blk, tile_exp, n_valid, w_gate, w_up, w_down, *, tm, tf):
    d = w_gate.shape[1]
    dc = d // LANES
    m = xs.shape[0] // dc
    f = w_gate.shape[2]
    nt = m // tm
    grid_spec = pltpu.PrefetchScalarGridSpec(
        num_scalar_prefetch=3,
        grid=(nt,),
        in_specs=[
            pl.BlockSpec((tm * dc, LANES), lambda i, blk, ex, nv: (blk[i], 0)),
            pl.BlockSpec((1, d, f), lambda i, blk, ex, nv: (ex[i], 0, 0)),
            pl.BlockSpec((1, d, f), lambda i, blk, ex, nv: (ex[i], 0, 0)),
            pl.BlockSpec((1, f, d), lambda i, blk, ex, nv: (ex[i], 0, 0)),
        ],
        out_specs=pl.BlockSpec((tm * dc, LANES), lambda i, blk, ex, nv: (i, 0)),
    )
    return pl.pallas_call(
        functools.partial(_expert_kernel, tm=tm, tf=tf),
        grid_spec=grid_spec,
        out_shape=jax.ShapeDtypeStruct((m * dc, LANES), F32),
        compiler_params=pltpu.CompilerParams(dimension_semantics=("arbitrary",)),
        name="experts",
    )(tile_blk, tile_exp, n_valid, xs, w_gate.astype(BF16), w_up.astype(BF16), w_down.astype(BF16))


def _router_kernel(a_ref, wo_ref, res_ref, g_ref, rt_ref,
                   x_ref, h_ref, idx_ref, gate_ref, rank_ref, cnt_ref, carry_ref):
    tm = res_ref.shape[0]
    ne = rt_ref.shape[0]

    @pl.when(pl.program_id(0) == 0)
    def _():
        carry_ref[...] = jnp.zeros_like(carry_ref)

    x = res_ref[...] + _dot(a_ref[...], wo_ref[...])
    x_ref[...] = x
    h = _rms(x, g_ref[...])
    _rows_to_slabs(h_ref, h)
    logits = _dot_nt(rt_ref[...], h.astype(BF16))
    eidx = lax.broadcasted_iota(jnp.int32, logits.shape, 0)
    m1 = jnp.max(logits, axis=0, keepdims=True)
    i1 = jnp.min(jnp.where(logits == m1, eidx, ne), axis=0, keepdims=True)
    rest = jnp.where(eidx == i1, -jnp.inf, logits)
    m2 = jnp.max(rest, axis=0, keepdims=True)
    i2 = jnp.min(jnp.where(rest == m2, eidx, ne), axis=0, keepdims=True)
    e2 = jnp.exp(m2 - m1)
    den = 1.0 + e2
    idx_ref[...] = jnp.concatenate([i1, i2], axis=0)
    gate_ref[...] = jnp.concatenate([1.0 / den, e2 / den], axis=0)

    sel1 = eidx == i1
    sel2 = eidx == i2
    member = jnp.where(sel1 | sel2, 1.0, 0.0)
    before = (lax.broadcasted_iota(jnp.int32, (tm, tm), 0) < lax.broadcasted_iota(jnp.int32, (tm, tm), 1))
    excl = _dot(member.astype(BF16), jnp.where(before, 1.0, 0.0).astype(BF16))
    excl = excl + carry_ref[:, 0:1]
    r1 = jnp.sum(jnp.where(sel1, excl, 0.0), axis=0, keepdims=True)
    r2 = jnp.sum(jnp.where(sel2, excl, 0.0), axis=0, keepdims=True)
    rank_ref[...] = jnp.concatenate([r1, r2], axis=0).astype(jnp.int32)
    total = carry_ref[:, 0:1] + jnp.sum(member, axis=1, keepdims=True)
    carry_ref[...] = jnp.broadcast_to(total, carry_ref.shape)
    cnt_ref[...] = jnp.broadcast_to(total, cnt_ref.shape).astype(jnp.int32)


def _outproj_router(a, w_out, res, gain, router, *, tm):
    n, d = res.shape
    k = a.shape[1]
    ne = router.shape[1]
    return pl.pallas_call(
        _router_kernel,
        grid=(n // tm,),
        in_specs=[
            pl.BlockSpec((tm, k), lambda i: (i, 0)),
            pl.BlockSpec((k, d), lambda i: (0, 0)),
            pl.BlockSpec((tm, d), lambda i: (i, 0)),
            pl.BlockSpec((1, d), lambda i: (0, 0)),
            pl.BlockSpec((ne, d), lambda i: (0, 0)),
        ],
        out_specs=[
            pl.BlockSpec((tm, d), lambda i: (i, 0)),
            pl.BlockSpec((tm * (d // LANES), LANES), lambda i: (i, 0)),
            pl.BlockSpec((TOP_K, tm), lambda i: (0, i)),
            pl.BlockSpec((TOP_K, tm), lambda i: (0, i)),
            pl.BlockSpec((TOP_K, tm), lambda i: (0, i)),
            pl.BlockSpec((ne, 128), lambda i: (0, 0)),
        ],
        out_shape=[
            jax.ShapeDtypeStruct((n, d), F32),
            jax.ShapeDtypeStruct((n * (d // LANES), LANES), F32),
            jax.ShapeDtypeStruct((TOP_K, n), jnp.int32),
            jax.ShapeDtypeStruct((TOP_K, n), F32),
            jax.ShapeDtypeStruct((TOP_K, n), jnp.int32),
            jax.ShapeDtypeStruct((ne, 128), jnp.int32),
        ],
        scratch_shapes=[pltpu.VMEM((ne, 128), F32)],
        compiler_params=pltpu.CompilerParams(dimension_semantics=("arbitrary",)),
        name="router",
    )(a, w_out.astype(BF16), res, gain.reshape(1, d), router.T.astype(BF16))


def _scatter_kernel(pad_start_ref, pad_len_ref, pos_ref, h_ref, xs_ref, zero_ref, sem, zsem, *, tm_exp):
    tm = pos_ref.shape[2]
    dc = h_ref.shape[0] // tm
    ne = pad_start_ref.shape[0] - 1
    bits = [1 << b for b in reversed(range((tm_exp - 1).bit_length()))]

    def zero_copy(first_row, nrows):
        dst = pl.multiple_of(first_row * dc, dc)
        return pltpu.make_async_copy(zero_ref.at[pl.ds(0, nrows * dc)], xs_ref.at[pl.ds(dst, nrows * dc)], zsem)

    def for_each_zero_copy(act):
        for e in range(ne):
            for bit in bits:
                @pl.when((pad_len_ref[e] & bit) != 0)
                def _(e=e, bit=bit):
                    higher = pad_len_ref[e] & ~(2 * bit - 1)
                    act(zero_copy(pad_start_ref[e] + higher, bit))
        for t in range(ne):
            @pl.when(t < pad_len_ref[ne])
            def _(t=t):
                act(zero_copy(pad_start_ref[ne] + t * tm_exp, tm_exp))

    def row_copy(r, slot):
        src = pl.multiple_of(r * dc, dc)
        dst = pl.multiple_of(pos_ref[0, slot, r] * dc, dc)
        return pltpu.make_async_copy(h_ref.at[pl.ds(src, dc)], xs_ref.at[pl.ds(dst, dc)], sem)

    def start(r, carry):
        for slot in range(TOP_K):
            row_copy(r, slot).start(priority=slot)
        return carry

    def wait(r, carry):
        for slot in range(TOP_K):
            row_copy(r, slot).wait()
        return carry

    @pl.when(pl.program_id(0) == 0)
    def _():
        zero_ref[...] = jnp.zeros_like(zero_ref)
        for_each_zero_copy(lambda c: c.start())

    lax.fori_loop(0, tm, start, 0, unroll=8)

    @pl.when(pl.program_id(0) == 0)
    def _():
        for_each_zero_copy(lambda c: c.wait())

    lax.fori_loop(0, tm, wait, 0, unroll=8)


def _scatter_rows(h, pos_tiles, pad_start, pad_len, m_rows, *, tm, tm_exp):
    n = pos_tiles.shape[0] * tm
    dc = h.shape[0] // n
    grid_spec = pltpu.PrefetchScalarGridSpec(
        num_scalar_prefetch=2,
        grid=(n // tm,),
        in_specs=[
            pl.BlockSpec((1, TOP_K, tm), lambda i, ps, pn: (i, 0, 0), memory_space=pltpu.SMEM),
            pl.BlockSpec((tm * dc, LANES), lambda i, ps, pn: (i, 0)),
        ],
        out_specs=pl.BlockSpec(memory_space=pl.ANY),
        scratch_shapes=[pltpu.VMEM((tm_exp * dc, LANES), F32), pltpu.SemaphoreType.DMA(()),
                        pltpu.SemaphoreType.DMA(())],
    )
    return pl.pallas_call(
        functools.partial(_scatter_kernel, tm_exp=tm_exp),
        grid_spec=grid_spec,
        out_shape=jax.ShapeDtypeStruct((m_rows * dc, LANES), F32),
        compiler_params=pltpu.CompilerParams(dimension_semantics=("arbitrary",), has_side_effects=True),
        name="scatter_rows",
    )(pad_start, pad_len, pos_tiles, h)


def _combine_kernel(pos_ref, pos_next_ref, x_ref, gate_ref, g_ref, ys_ref, out_ref, rows_ref, sems):
    tm = x_ref.shape[0]
    i = pl.program_id(0)
    n_steps = pl.num_programs(0)
    buf = i % 2

    dc = rows_ref.shape[2] // tm

    def row_copy(p_ref, b, r, slot):
        src = pl.multiple_of(p_ref[0, slot, r] * dc, dc)
        dst = pl.multiple_of(r * dc, dc)
        return pltpu.make_async_copy(ys_ref.at[pl.ds(src, dc)], rows_ref.at[b, slot, pl.ds(dst, dc)], sems.at[b])

    def start_tile(p_ref, b):
        def body(r, carry):
            for slot in range(TOP_K):
                row_copy(p_ref, b, r, slot).start(priority=slot)
            return carry
        lax.fori_loop(0, tm, body, 0, unroll=8)

    @pl.when(i == 0)
    def _():
        start_tile(pos_ref, 0)

    @pl.when(i + 1 < n_steps)
    def _():
        start_tile(pos_next_ref, 1 - buf)

    def wait(r, carry):
        for slot in range(TOP_K):
            row_copy(pos_ref, buf, r, slot).wait()
        return carry

    lax.fori_loop(0, tm, wait, 0, unroll=8)
    y = (x_ref[...] + gate_ref[:, 0:1] * _slabs_to_rows(rows_ref.at[buf, 0], tm)
         + gate_ref[:, 1:2] * _slabs_to_rows(rows_ref.at[buf, 1], tm))
    out_ref[...] = _rms(y, g_ref[...])


def _combine(x2, pos_tiles, gate_col, gain, ys, *, tm):
    n, d = x2.shape
    dc = d // LANES
    nt = n // tm
    return pl.pallas_call(
        _combine_kernel,
        grid=(nt,),
        in_specs=[
            pl.BlockSpec((1, TOP_K, tm), lambda i: (i, 0, 0), memory_space=pltpu.SMEM),
            pl.BlockSpec((1, TOP_K, tm), lambda i: (jnp.minimum(i + 1, nt - 1), 0, 0), memory_space=pltpu.SMEM),
            pl.BlockSpec((tm, d), lambda i: (i, 0)),
            pl.BlockSpec((tm, TOP_K), lambda i: (i, 0)),
            pl.BlockSpec((1, d), lambda i: (0, 0)),
            pl.BlockSpec(memory_space=pl.ANY),
        ],
        out_specs=pl.BlockSpec((tm, d), lambda i: (i, 0)),
        out_shape=jax.ShapeDtypeStruct((n, d), F32),
        scratch_shapes=[pltpu.VMEM((2, TOP_K, tm * dc, LANES), F32), pltpu.SemaphoreType.DMA((2,))],
        compiler_params=pltpu.CompilerParams(dimension_semantics=("arbitrary",)),
        name="combine",
    )(pos_tiles, pos_tiles, x2, gate_col, gain.reshape(1, d), ys)


def _moe_and_final_norm(a, w_out, res, norm_gain, router, w_gate, w_up, w_down, final_gain,
                        *, tm_route, tm_exp, tf):
    n, d = res.shape
    ne = router.shape[1]
    x2, h, idx_t, gate_t, rank_t, counts = _outproj_router(a, w_out, res, norm_gain, router, tm=tm_route)

    cnt = counts[:, 0]
    padded = ((cnt + tm_exp - 1) // tm_exp) * tm_exp
    ends = jnp.cumsum(padded)
    starts = ends - padded
    pos = rank_t
    for e in range(ne):
        pos = pos + jnp.where(idx_t == e, starts[e], 0)
    pos_tiles = pos.reshape(TOP_K, n // tm_route, tm_route).transpose(1, 0, 2)
    m_rows = TOP_K * n + ne * tm_exp
    nt = m_rows // tm_exp
    n_valid = (ends[-1] // tm_exp).astype(jnp.int32)
    tile_blk = jnp.minimum(jnp.arange(nt, dtype=jnp.int32), n_valid - 1)
    tile_exp = jnp.minimum(
        jnp.sum(ends[None, :] <= (tile_blk * tm_exp)[:, None], axis=1).astype(jnp.int32), ne - 1)

    pad_start = jnp.concatenate([starts + cnt, ends[-1:]]).astype(jnp.int32)
    pad_len = jnp.concatenate([padded - cnt, (nt - n_valid).reshape(1)]).astype(jnp.int32)
    xs = _scatter_rows(h, pos_tiles, pad_start, pad_len, m_rows, tm=tm_route, tm_exp=tm_exp)
    ys = _experts(xs, tile_blk, tile_exp, n_valid.reshape(1), w_gate, w_up, w_down, tm=tm_exp, tf=tf)
    return _combine(x2, pos_tiles, gate_t.T, final_gain, ys, tm=tm_route)


def _trunk(x, l0_norm_mix, l0_ml_w_in, l0_ml_b_if, l0_ml_head_gain, l0_ml_w_out,
           l0_norm_ffn, l0_ffn_w_gate, l0_ffn_w_up, l0_ffn_w_down,
           l1_norm_mix, l1_gdn_w_in, l1_gdn_conv, l1_gdn_a_log, l1_gdn_dt_bias,
           l1_gdn_out_gain, l1_gdn_w_out, l1_norm_ffn, l1_moe_router,
           l1_moe_w_gate, l1_moe_w_up, l1_moe_w_down, final_norm, *, tiles):
    bsz, s, d = x.shape
    n = bsz * s
    nc = s // CHUNK
    x2 = x.reshape(n, d)

    qk_w = ML_HEADS * ML_DQK
    v_w = ML_HEADS * ML_DV
    fm = 2 * qk_w + 2 * v_w
    col_scale = jnp.concatenate([jnp.ones((qk_w,), F32), jnp.full((qk_w,), ML_DQK ** -0.5, F32),
                                 jnp.ones((2 * v_w,), F32)])
    ng = 2 * ML_HEADS
    p, gcol, grow = _inproj(x2, l0_norm_mix, l0_ml_w_in[:, :fm] * col_scale, l0_ml_w_in[:, fm:],
                            l0_ml_b_if, jnp.zeros((ng,), F32), _mlstm_gate_act,
                            tm=tiles["tm_proj"], tn=tiles["tn_proj"])
    grow = grow.reshape(ng, bsz, nc, CHUNK).transpose(1, 2, 0, 3)
    hs = _mlstm(p.reshape(bsz, s, fm), gcol.reshape(bsz, s, ng), grow, l0_ml_head_gain, bsz, s,
                bb=tiles["bb_ml"])

    x2 = _outproj_ffn(hs.reshape(n, v_w), l0_ml_w_out, x2, l0_norm_ffn, l0_ffn_w_gate, l0_ffn_w_up,
                      l0_ffn_w_down, tm=tiles["tm_ffn"], tf=tiles["tf"])

    kw = GD_HEADS * GD_DK
    vw = GD_HEADS * GD_DV
    fm = 2 * kw + 2 * vw
    ng = 2 * GD_HEADS
    zeros_h = jnp.zeros((GD_HEADS,), F32)
    p, gcol, grow = _inproj(x2, l1_norm_mix, l1_gdn_w_in[:, :fm], l1_gdn_w_in[:, fm:],
                            jnp.concatenate([zeros_h, l1_gdn_dt_bias]),
                            jnp.concatenate([zeros_h, l1_gdn_a_log]), _gdn_gate_act,
                            tm=tiles["tm_proj"], tn=tiles["tn_proj"])
    grow = _gdn_pair_rows(grow, bsz, nc)
    o = _gdn(p.reshape(bsz, s, fm), gcol.reshape(bsz, s, ng), grow, l1_gdn_conv, l1_gdn_out_gain, bsz, s,
             bb=tiles["bb_gd"])

    out = _moe_and_final_norm(o.reshape(n, vw), l1_gdn_w_out, x2, l1_norm_ffn, l1_moe_router,
                              l1_moe_w_gate, l1_moe_w_up, l1_moe_w_down, final_norm,
                              tm_route=tiles["tm_route"], tm_exp=tiles["tm_exp"], tf=tiles["tf_exp"])
    return out.reshape(bsz, s, d)


def _tiles_for(n, bsz, f):
    cap = lambda t: min(t, n)
    return dict(tm_proj=cap(512), tn_proj=1024, tm_ffn=cap(512), tf=f // 2,
                tm_route=cap(512), tm_exp=cap(256), tf_exp=f // 2, bb_ml=min(4, bsz), bb_gd=min(4, bsz))


def kernel(x, l0_norm_mix, l0_ml_w_in, l0_ml_b_if, l0_ml_head_gain, l0_ml_w_out, l0_norm_ffn, l0_ffn_w_gate, l0_ffn_w_up, l0_ffn_w_down, l1_norm_mix, l1_gdn_w_in, l1_gdn_conv, l1_gdn_a_log, l1_gdn_dt_bias, l1_gdn_out_gain, l1_gdn_w_out, l1_norm_ffn, l1_moe_router, l1_moe_w_gate, l1_moe_w_up, l1_moe_w_down, final_norm):
    n = x.shape[0] * x.shape[1]
    return _trunk(x, l0_norm_mix, l0_ml_w_in, l0_ml_b_if, l0_ml_head_gain, l0_ml_w_out,
                  l0_norm_ffn, l0_ffn_w_gate, l0_ffn_w_up, l0_ffn_w_down,
                  l1_norm_mix, l1_gdn_w_in, l1_gdn_conv, l1_gdn_a_log, l1_gdn_dt_bias,
                  l1_gdn_out_gain, l1_gdn_w_out, l1_norm_ffn, l1_moe_router,
                  l1_moe_w_gate, l1_moe_w_up, l1_moe_w_down, final_norm, tiles=_tiles_for(n, x.shape[0], l0_ffn_w_gate.shape[1]))
```

```python
import functools

import jax
import jax.numpy as jnp
from jax import lax
from jax.experimental import pallas as pl
from jax.experimental.pallas import tpu as pltpu

EPS = 1e-6
CHUNK = 64

ML_HEADS, ML_DQK, ML_DV = 4, 128, 256
GD_HEADS, GD_DK, GD_DV = 8, 128, 128
GD_CONV_K = 4
N_EXPERTS, TOP_K = 8, 2

F32 = jnp.float32
BF16 = jnp.bfloat16

_NT = (((1,), (1,)), ((), ()))
_TN = (((0,), (0,)), ((), ()))


def _dot(a, b):
    return jnp.dot(a, b, preferred_element_type=F32)


def _dot_nt(a, b):
    return lax.dot_general(a, b, _NT, preferred_element_type=F32)


def _dot_tn(a, b):
    return lax.dot_general(a, b, _TN, preferred_element_type=F32)


def _rms(x, g):
    return x * lax.rsqrt(jnp.mean(x * x, axis=-1, keepdims=True) + EPS) * g


def _softplus(x):
    return jnp.maximum(x, 0.0) + jnp.log(1.0 + jnp.exp(-jnp.abs(x)))


def _sigmoid(x):
    return 1.0 / (1.0 + jnp.exp(-x))


def _silu(x):
    return x * _sigmoid(x)


LANES = 128


def _rows_to_slabs(ref, x):
    rows, d = x.shape
    dc = d // LANES
    for c in range(dc):
        ref[pl.ds(c, rows, stride=dc), :] = x[:, c * LANES:(c + 1) * LANES]


def _slabs_to_rows(ref, rows):
    dc = ref.shape[0] // rows
    return jnp.concatenate([ref[pl.ds(c, rows, stride=dc), :] for c in range(dc)], axis=1)


def _mlstm_gate_act(pre, gidx, p1, p2):
    del p2
    z = pre + p1
    return jnp.where(gidx < ML_HEADS, z, -_softplus(-z))


def _gdn_gate_act(pre, gidx, p1, p2):
    beta = _sigmoid(pre)
    g = -(jnp.exp(p2) * _softplus(pre + p1))
    return jnp.where(gidx < GD_HEADS, beta, g)


def _cast_slices(src_refs, dst_refs):
    for src, dst in zip(src_refs, dst_refs):
        dst[...] = src[...].astype(dst.dtype)


def _cast_specs(arrays, steps, index_map):
    specs = [pl.BlockSpec((a.shape[0] // steps, a.shape[1]), index_map) for a in arrays]
    shapes = [jax.ShapeDtypeStruct(a.shape, BF16) for a in arrays]
    return specs, shapes


def _inproj_kernel(*refs, act, tn, prenormed, n_cast):
    x_ref, g_ref, w_ref, wg_ref, wgt_ref, pc_ref, pr_ref = refs[:7]
    cast_src = refs[7:7 + n_cast]
    main_ref, gcol_ref, grow_ref = refs[7 + n_cast:10 + n_cast]
    cast_dst = refs[10 + n_cast:]
    h = x_ref[...] if prenormed else _rms(x_ref[...], g_ref[...]).astype(BF16)
    pre_c = _dot(h, wg_ref[...])
    gi_c = lax.broadcasted_iota(jnp.int32, pre_c.shape, 1)
    gcol_ref[...] = act(pre_c, gi_c, pc_ref[0:1, :], pc_ref[1:2, :])
    pre_r = _dot_nt(wgt_ref[...], h)
    gi_r = lax.broadcasted_iota(jnp.int32, pre_r.shape, 0)
    grow_ref[...] = act(pre_r, gi_r, pr_ref[:, 0:1], pr_ref[:, 1:2])
    for c in range(0, main_ref.shape[1], tn):
        main_ref[:, c:c + tn] = _dot(h, w_ref[:, c:c + tn]).astype(main_ref.dtype)
    _cast_slices(cast_src, cast_dst)


def _inproj(x2, gain, w_all, fm, p1, p2, act, *, tm, tn, prenormed=False, casts=()):
    n, d = x2.shape
    ng = w_all.shape[1] - fm
    w_gate = w_all[:, fm:]
    pc = jnp.stack([p1, p2]).astype(F32)
    pr = pc.T
    steps = n // tm
    cast_specs, cast_shapes = _cast_specs(casts, steps, lambda i: (i, 0))
    return pl.pallas_call(
        functools.partial(_inproj_kernel, act=act, tn=tn, prenormed=prenormed, n_cast=len(casts)),
        grid=(steps,),
        in_specs=[
            pl.BlockSpec((tm, d), lambda i: (i, 0)),
            pl.BlockSpec((1, d), lambda i: (0, 0)),
            pl.BlockSpec((d, fm), lambda i: (0, 0)),
            pl.BlockSpec((d, ng), lambda i: (0, 0)),
            pl.BlockSpec((ng, d), lambda i: (0, 0)),
            pl.BlockSpec((2, ng), lambda i: (0, 0)),
            pl.BlockSpec((ng, 2), lambda i: (0, 0)),
        ] + cast_specs,
        out_specs=[
            pl.BlockSpec((tm, fm), lambda i: (i, 0)),
            pl.BlockSpec((tm, ng), lambda i: (i, 0)),
            pl.BlockSpec((ng, tm), lambda i: (0, i)),
        ] + cast_specs,
        out_shape=[
            jax.ShapeDtypeStruct((n, fm), BF16),
            jax.ShapeDtypeStruct((n, ng), F32),
            jax.ShapeDtypeStruct((ng, n), F32),
        ] + cast_shapes,
        compiler_params=pltpu.CompilerParams(dimension_semantics=("parallel",)),
        name="inproj",
    )(x2, gain.reshape(1, d), w_all, w_gate, w_gate.T, pc, pr, *casts)


def _cumsum_col_row(v_col, v_row, incl_lower, incl_upper):
    c_col = jnp.sum(jnp.where(incl_lower, v_row, 0.0), axis=1, keepdims=True)
    c_row = jnp.sum(jnp.where(incl_upper, v_col, 0.0), axis=0, keepdims=True)
    return c_col, c_row


def _mlstm_kernel(q_ref, k_ref, v_ref, o_ref, gcol_ref, grow_ref, gain_ref, cast_src, out_ref, cast_dst,
                  c_ref, n_ref, m_ref):
    L = CHUNK
    H = ML_HEADS
    _cast_slices([cast_src], [cast_dst])

    @pl.when(pl.program_id(1) == 0)
    def _():
        c_ref[...] = jnp.zeros_like(c_ref)
        n_ref[...] = jnp.zeros_like(n_ref)
        m_ref[...] = jnp.zeros_like(m_ref)

    row = lax.broadcasted_iota(jnp.int32, (L, L), 0)
    col = lax.broadcasted_iota(jnp.int32, (L, L), 1)
    lower = col <= row
    upper = row <= col
    units = [(bi, h) for bi in range(q_ref.shape[0]) for h in range(H)]
    hs = range(len(units))
    q = [q_ref[bi, :, h * ML_DQK:(h + 1) * ML_DQK] for bi, h in units]
    k = [k_ref[bi, :, h * ML_DQK:(h + 1) * ML_DQK] for bi, h in units]
    v = [v_ref[bi, :, h * ML_DV:(h + 1) * ML_DV] for bi, h in units]
    i_col = [gcol_ref[bi, :, h:h + 1] for bi, h in units]
    i_row = [grow_ref[bi, 0, h:h + 1, :] for bi, h in units]
    b = [_cumsum_col_row(gcol_ref[bi, :, H + h:H + h + 1], grow_ref[bi, 0, H + h:H + h + 1, :], lower, upper)
         for bi, h in units]
    b_col = [c for c, _ in b]
    b_row = [r for _, r in b]
    m_prev = [m_ref[h, 0:1, 0:1] for h in hs]
    c_prev = [c_ref[h] for h in hs]
    n_prev = [n_ref[h, 0:1, :] for h in hs]

    qk = [_dot_nt(q[h], k[h]) for h in hs]
    qc = [_dot(q[h], c_prev[h].astype(BF16)) for h in hs]
    d = [jnp.where(lower, b_col[h] - b_row[h] + i_row[h], -jnp.inf) for h in hs]
    m_inter = [b_col[h] + m_prev[h] for h in hs]
    m_t = [jnp.maximum(jnp.max(d[h], axis=1, keepdims=True), m_inter[h]) for h in hs]
    sc = [qk[h] * jnp.exp(d[h] - m_t[h]) for h in hs]
    a = [jnp.exp(m_inter[h] - m_t[h]) for h in hs]
    num = [_dot(sc[h].astype(BF16), v[h]) + a[h] * qc[h] for h in hs]
    den = [jnp.sum(sc[h], axis=1, keepdims=True)
           + a[h] * jnp.sum(q[h].astype(F32) * n_prev[h], axis=1, keepdims=True) for h in hs]
    hc = [num[h] / jnp.maximum(jnp.abs(den[h]), jnp.exp(-m_t[h])) for h in hs]

    b_last = [c[L - 1:L, :] for c in b_col]
    m_new = [jnp.maximum(b_last[h] + m_prev[h],
                         jnp.max(b_last[h] - b_row[h] + i_row[h], axis=1, keepdims=True)) for h in hs]
    ws_col = [jnp.exp(b_last[h] - b_col[h] + i_col[h] - m_new[h]) for h in hs]
    decay = [jnp.exp(b_last[h] + m_prev[h] - m_new[h]) for h in hs]
    wv = [(ws_col[h] * v[h].astype(F32)).astype(BF16) for h in hs]
    kv = [_dot_tn(k[h], wv[h]) for h in hs]
    for h in hs:
        c_ref[h] = decay[h] * c_prev[h] + kv[h]
        n_new = decay[h] * n_prev[h] + jnp.sum(ws_col[h] * k[h].astype(F32), axis=0, keepdims=True)
        n_ref[h] = jnp.broadcast_to(n_new, n_ref.shape[1:])
        m_ref[h] = jnp.broadcast_to(m_new[h], m_ref.shape[1:])
    for u, (bi, h) in enumerate(units):
        og = o_ref[bi, :, h * ML_DV:(h + 1) * ML_DV].astype(F32)
        gain = gain_ref[0:1, h * ML_DV:(h + 1) * ML_DV]
        y = _rms(hc[u], gain) * _sigmoid(og)
        out_ref[bi, :, h * ML_DV:(h + 1) * ML_DV] = y.astype(out_ref.dtype)


def _mlstm(p, gcol, grow, head_gain, bsz, s, cast, *, bb):
    nc = s // CHUNK
    qk_w = ML_HEADS * ML_DQK
    v_w = ML_HEADS * ML_DV
    ng = 2 * ML_HEADS
    (cast_spec,), (cast_shape,) = _cast_specs([cast], (bsz // bb) * nc, lambda b, c: (b * nc + c, 0))
    return pl.pallas_call(
        _mlstm_kernel,
        grid=(bsz // bb, nc),
        in_specs=[
            pl.BlockSpec((bb, CHUNK, qk_w), lambda b, c: (b, c, 0)),
            pl.BlockSpec((bb, CHUNK, qk_w), lambda b, c: (b, c, 1)),
            pl.BlockSpec((bb, CHUNK, v_w), lambda b, c: (b, c, 1)),
            pl.BlockSpec((bb, CHUNK, v_w), lambda b, c: (b, c, 2)),
            pl.BlockSpec((bb, CHUNK, ng), lambda b, c: (b, c, 0)),
            pl.BlockSpec((bb, 1, ng, CHUNK), lambda b, c: (b, c, 0, 0)),
            pl.BlockSpec((1, v_w), lambda b, c: (0, 0)),
            cast_spec,
        ],
        out_specs=[pl.BlockSpec((bb, CHUNK, v_w), lambda b, c: (b, c, 0)), cast_spec],
        out_shape=[jax.ShapeDtypeStruct((bsz, s, v_w), BF16), cast_shape],
        scratch_shapes=[
            pltpu.VMEM((bb * ML_HEADS, ML_DQK, ML_DV), F32),
            pltpu.VMEM((bb * ML_HEADS, 8, ML_DQK), F32),
            pltpu.VMEM((bb * ML_HEADS, 8, 128), F32),
        ],
        compiler_params=pltpu.CompilerParams(dimension_semantics=("parallel", "arbitrary")),
        name="mlstm",
    )(p, p, p, p, gcol, grow, head_gain.reshape(1, v_w).astype(F32), cast)


def _pair_block_diag(xp, left):
    zero = jnp.zeros_like(xp)
    return jnp.concatenate([jnp.where(left, xp, zero), jnp.where(left, zero, xp)], axis=0)


def _unit_lower_inverse_pairs(a_list, left, eye):
    L = a_list[0].shape[0]
    xs = [-a for a in a_list]
    ps = [eye + x for x in xs]
    steps = max(1, (L - 1).bit_length()) - 1
    for _ in range(steps):
        x16 = [x.astype(BF16) for x in xs]
        xs = [_dot(xb, _pair_block_diag(xb, left)) for xb in x16]
        xbd = [_pair_block_diag(x.astype(BF16), left) for x in xs]
        ps = [p + _dot(p.astype(BF16), xb) for p, xb in zip(ps, xbd)]
    return ps


def _gdn_kernel(q_ref, k_ref, v_ref, z_ref, gcol_ref, grow_ref, cw_ref, gain_ref, out_ref,
                s_ref, halo_ref):
    L = CHUNK
    H = GD_HEADS
    kw = H * GD_DK

    @pl.when(pl.program_id(1) == 0)
    def _():
        s_ref[...] = jnp.zeros_like(s_ref)
        halo_ref[...] = jnp.zeros_like(halo_ref)

    halo_rows = halo_ref.shape[2]
    nsh = GD_CONV_K - 1
    rr = lax.broadcasted_iota(jnp.int32, (nsh * L, halo_rows + L), 0)
    cc = lax.broadcasted_iota(jnp.int32, (nsh * L, halo_rows + L), 1)
    shifts = jnp.where(cc == halo_rows + (rr % L) - (rr // L + 1), 1.0, 0.0).astype(BF16)

    def conv_silu(x_ref, part, bi):
        xb = x_ref[bi]
        width = xb.shape[1]
        ext = jnp.concatenate([halo_ref[bi, part, :, 0:width], xb], axis=0)
        r = _dot(shifts, ext)
        cw = cw_ref[:, part * kw:part * kw + width]
        acc = xb.astype(F32) * cw[GD_CONV_K - 1:GD_CONV_K, :]
        for sft in range(1, GD_CONV_K):
            acc = acc + r[(sft - 1) * L:sft * L, :] * cw[GD_CONV_K - 1 - sft:GD_CONV_K - sft, :]
        halo_ref[bi, part, :, 0:width] = xb[L - halo_rows:L, :]
        return _silu(acc)

    bb = q_ref.shape[0]
    qa = [conv_silu(q_ref, 0, bi) for bi in range(bb)]
    ka = [conv_silu(k_ref, 1, bi) for bi in range(bb)]
    va = [conv_silu(v_ref, 2, bi) for bi in range(bb)]

    lane = lax.broadcasted_iota(jnp.int32, (L, 2 * L), 1)
    rowp = lax.broadcasted_iota(jnp.int32, (L, 2 * L), 0)
    left = lane < L
    colp = jnp.where(left, lane, lane - L)
    lower_p = colp <= rowp
    upper_p = rowp <= colp
    strict_p = colp < rowp
    eye_p = jnp.where(colp == rowp, 1.0, 0.0)
    zeros_k = jnp.zeros((L, GD_DK), F32)
    zeros_s = jnp.zeros((GD_DK, GD_DV), F32)

    def side_by_side(x0, x1):
        return jnp.concatenate([jnp.concatenate([x0, zeros_k], axis=1),
                                jnp.concatenate([zeros_k, x1], axis=1)], axis=0)

    units = [(bi, hp) for bi in range(bb) for hp in range(H // 2)]
    us = range(len(units))
    heads = [(bi, 2 * hp + j) for bi, hp in units for j in range(2)]
    beta = [gcol_ref[bi, :, h:h + 1] for bi, h in heads]
    g_col = [gcol_ref[bi, :, H + h:H + h + 1] for bi, h in heads]
    g_row_p = [grow_ref[bi, 0, hp:hp + 1, :] for bi, hp in units]
    g_col_p = [jnp.where(left, g_col[2 * u], g_col[2 * u + 1]) for u in us]
    gc_row_p = [jnp.sum(jnp.where(upper_p, g_col_p[u], 0.0), axis=0, keepdims=True) for u in us]
    part = [jnp.where(lower_p, g_row_p[u], 0.0) for u in us]
    gc_col = [jnp.sum(jnp.where(left if j == 0 else ~left, part[u], 0.0), axis=1, keepdims=True)
              for u in us for j in range(2)]
    gc_col_p = [jnp.where(left, gc_col[2 * u], gc_col[2 * u + 1]) for u in us]
    gamma_p = [jnp.where(lower_p, jnp.exp(jnp.where(lower_p, gc_col_p[u] - gc_row_p[u], 0.0)), 0.0) for u in us]
    eg = [jnp.exp(c) for c in gc_col]
    g_last = [c[L - 1:L, :] for c in gc_col]

    q = [qa[bi][:, h * GD_DK:(h + 1) * GD_DK] for bi, h in heads]
    k = [ka[bi][:, h * GD_DK:(h + 1) * GD_DK] for bi, h in heads]
    v = [va[bi][:, h * GD_DV:(h + 1) * GD_DV] for bi, h in heads]
    q = [x * lax.rsqrt(jnp.sum(x * x, axis=1, keepdims=True) + EPS) * (GD_DK ** -0.5) for x in q]
    k = [x * lax.rsqrt(jnp.sum(x * x, axis=1, keepdims=True) + EPS) for x in k]
    kb = [k[h] * beta[h] for h in range(len(heads))]

    kq = [_dot_nt(jnp.concatenate([jnp.concatenate([kb[2 * u], q[2 * u]], axis=0),
                                   jnp.concatenate([kb[2 * u + 1], q[2 * u + 1]], axis=0)], axis=1).astype(BF16),
                  side_by_side(k[2 * u], k[2 * u + 1]).astype(BF16)) for u in us]
    a_p = [jnp.where(strict_p, kq[u][0:L] * gamma_p[u], 0.0) for u in us]
    attn_p = [(kq[u][L:2 * L] * gamma_p[u]).astype(BF16) for u in us]
    t_p = [x.astype(BF16) for x in _unit_lower_inverse_pairs(a_p, left, eye_p)]
    uw = [_dot(t_p[u], jnp.concatenate(
        [jnp.concatenate([v[2 * u] * beta[2 * u], kb[2 * u] * eg[2 * u], zeros_k, zeros_k], axis=1),
         jnp.concatenate([zeros_k, zeros_k, v[2 * u + 1] * beta[2 * u + 1], kb[2 * u + 1] * eg[2 * u + 1]], axis=1)],
        axis=0).astype(BF16)) for u in us]

    s_prev = [s_ref[h] for h in range(len(heads))]
    s_bd = [jnp.concatenate([jnp.concatenate([s_prev[2 * u], zeros_s], axis=1),
                             jnp.concatenate([zeros_s, s_prev[2 * u + 1]], axis=1)], axis=0).astype(BF16) for u in us]
    w_cols = (GD_DV, 2 * GD_DV + GD_DK)
    ws = [_dot(jnp.concatenate(
        [jnp.concatenate([uw[u][:, w_cols[0]:w_cols[0] + GD_DK], uw[u][:, w_cols[1]:w_cols[1] + GD_DK]], axis=1),
         jnp.concatenate([q[2 * u] * eg[2 * u], q[2 * u + 1] * eg[2 * u + 1]], axis=1)], axis=0).astype(BF16),
        s_bd[u]) for u in us]
    u_cols = (0, GD_DV + GD_DK)
    vn = [uw[u][:, u_cols[j]:u_cols[j] + GD_DV] - ws[u][0:L, j * GD_DV:(j + 1) * GD_DV]
          for u in us for j in range(2)]
    o_p = [ws[u][L:2 * L] + _dot(attn_p[u], side_by_side(vn[2 * u], vn[2 * u + 1]).astype(BF16)) for u in us]
    kdec = [(k[h] * jnp.exp(g_last[h] - gc_col[h])).astype(BF16) for h in range(len(heads))]
    for h in range(len(heads)):
        s_ref[h] = jnp.exp(g_last[h]) * s_prev[h] + _dot_tn(kdec[h], vn[h].astype(BF16))
    for i, (bi, h) in enumerate(heads):
        z = z_ref[bi, :, h * GD_DV:(h + 1) * GD_DV].astype(F32)
        o = o_p[i // 2][:, (i % 2) * GD_DV:(i % 2 + 1) * GD_DV]
        y = _rms(o, gain_ref[...]) * _silu(z)
        out_ref[bi, :, h * GD_DV:(h + 1) * GD_DV] = y.astype(out_ref.dtype)


def _gdn_pair_rows(grow, bsz, nc):
    hp = GD_HEADS // 2
    g = grow[GD_HEADS:].reshape(hp, 2, bsz, nc, CHUNK)
    return g.transpose(2, 3, 0, 1, 4).reshape(bsz, nc, hp, 2 * CHUNK)


def _gdn(p, gcol, grow, conv_w, out_gain, bsz, s, *, bb):
    nc = s // CHUNK
    kw = GD_HEADS * GD_DK
    vw = GD_HEADS * GD_DV
    ng = 2 * GD_HEADS
    cc = conv_w.shape[1]
    hp = GD_HEADS // 2
    return pl.pallas_call(
        _gdn_kernel,
        grid=(bsz // bb, nc),
        in_specs=[
            pl.BlockSpec((bb, CHUNK, kw), lambda b, c: (b, c, 0)),
            pl.BlockSpec((bb, CHUNK, kw), lambda b, c: (b, c, 1)),
            pl.BlockSpec((bb, CHUNK, vw), lambda b, c: (b, c, 2)),
            pl.BlockSpec((bb, CHUNK, vw), lambda b, c: (b, c, 3)),
            pl.BlockSpec((bb, CHUNK, ng), lambda b, c: (b, c, 0)),
            pl.BlockSpec((bb, 1, hp, 2 * CHUNK), lambda b, c: (b, c, 0, 0)),
            pl.BlockSpec((GD_CONV_K, cc), lambda b, c: (0, 0)),
            pl.BlockSpec((1, GD_DV), lambda b, c: (0, 0)),
        ],
        out_specs=pl.BlockSpec((bb, CHUNK, vw), lambda b, c: (b, c, 0)),
        out_shape=jax.ShapeDtypeStruct((bsz, s, vw), BF16),
        scratch_shapes=[
            pltpu.VMEM((bb * GD_HEADS, GD_DK, GD_DV), F32),
            pltpu.VMEM((bb, 3, 16, kw), BF16),
        ],
        compiler_params=pltpu.CompilerParams(dimension_semantics=("parallel", "arbitrary")),
        name="gdn",
    )(p, p, p, p, gcol, grow, conv_w.astype(F32), out_gain.reshape(1, GD_DV).astype(F32))


def _ffn_kernel(a_ref, wo_ref, res_ref, g_ref, gn_ref, wg_ref, wu_ref, wd_ref, out_ref, hn_ref, *, tf):
    x1 = res_ref[...] + _dot(a_ref[...], wo_ref[...])
    h = _rms(x1, g_ref[...]).astype(BF16)
    y = x1
    for c in range(0, wg_ref.shape[1], tf):
        a = _silu(_dot(h, wg_ref[:, c:c + tf])) * _dot(h, wu_ref[:, c:c + tf])
        y = y + _dot(a.astype(BF16), wd_ref[c:c + tf, :])
    out_ref[...] = y
    hn_ref[...] = _rms(y, gn_ref[...]).astype(hn_ref.dtype)


def _outproj_ffn(a, w_out, res, gain, w_gate, w_up, w_down, next_gain, *, tm, tf):
    n, d = res.shape
    k = a.shape[1]
    f = w_gate.shape[1]
    once = dict(pipeline_mode=pl.Buffered(1))
    return pl.pallas_call(
        functools.partial(_ffn_kernel, tf=tf),
        grid=(n // tm,),
        in_specs=[
            pl.BlockSpec((tm, k), lambda i: (i, 0)),
            pl.BlockSpec((k, d), lambda i: (0, 0), **once),
            pl.BlockSpec((tm, d), lambda i: (i, 0)),
            pl.BlockSpec((1, d), lambda i: (0, 0)),
            pl.BlockSpec((1, d), lambda i: (0, 0)),
            pl.BlockSpec((d, f), lambda i: (0, 0), **once),
            pl.BlockSpec((d, f), lambda i: (0, 0), **once),
            pl.BlockSpec((f, d), lambda i: (0, 0), **once),
        ],
        out_specs=[pl.BlockSpec((tm, d), lambda i: (i, 0)), pl.BlockSpec((tm, d), lambda i: (i, 0))],
        out_shape=[jax.ShapeDtypeStruct((n, d), F32), jax.ShapeDtypeStruct((n, d), BF16)],
        compiler_params=pltpu.CompilerParams(dimension_semantics=("parallel",)),
        name="ffn",
    )(a, w_out.astype(BF16), res, gain.reshape(1, d), next_gain.reshape(1, d), w_gate.astype(BF16),
      w_up.astype(BF16), w_down.astype(BF16))


def _expert_kernel(blk_ref, exp_ref, nvalid_ref, x_ref, wg_ref, wu_ref, wd_ref, out_ref, *, tm, tf):
    del blk_ref, exp_ref
    i = pl.program_id(0)
    f = wg_ref.shape[2]

    @pl.when(i < nvalid_ref[0])
    def _():
        h = _slabs_to_rows(x_ref, tm).astype(BF16)
        y = None
        for c in range(0, f, tf):
            a = _silu(_dot(h, wg_ref[0, :, c:c + tf])) * _dot(h, wu_ref[0, :, c:c + tf])
            part = _dot(a.astype(BF16), wd_ref[0, c:c + tf, :])
            y = part if y is None else y + part
        _rows_to_slabs(out_ref, y)

    @pl.when(i >= nvalid_ref[0])
    def _():
        out_ref[...] = jnp.zeros_like(out_ref)


def _experts(xs, tile_blk, tile_exp, n_valid, w_gate, w_up, w_down, *, tm, tf):
    d = w_gate.shape[1]
    dc = d // LANES
    m = xs.shape[0] // dc
    f = w_gate.shape[2]
    nt = m // tm
    grid_spec = pltpu.PrefetchScalarGridSpec(
        num_scalar_prefetch=3,
        grid=(nt,),
        in_specs=[
            pl.BlockSpec((tm * dc, LANES), lambda i, blk, ex, nv: (blk[i], 0)),
            pl.BlockSpec((1, d, f), lambda i, blk, ex, nv: (ex[i], 0, 0)),
            pl.BlockSpec((1, d, f), lambda i, blk, ex, nv: (ex[i], 0, 0)),
            pl.BlockSpec((1, f, d), lambda i, blk, ex, nv: (ex[i], 0, 0)),
        ],
        out_specs=pl.BlockSpec((tm * dc, LANES), lambda i, blk, ex, nv: (i, 0)),
    )
    return pl.pallas_call(
        functools.partial(_expert_kernel, tm=tm, tf=tf),
        grid_spec=grid_spec,
        out_shape=jax.ShapeDtypeStruct((m * dc, LANES), F32),
        compiler_params=pltpu.CompilerParams(dimension_semantics=("arbitrary",)),
        name="experts",
    )(tile_blk, tile_exp, n_valid, xs, w_gate.astype(BF16), w_up.astype(BF16), w_down.astype(BF16))


def _router_kernel(a_ref, wo_ref, res_ref, g_ref, rt_ref,
                   x_ref, h_ref, idx_ref, gate_ref, rank_ref, cnt_ref, carry_ref):
    tm = res_ref.shape[0]
    ne = rt_ref.shape[0]

    @pl.when(pl.program_id(0) == 0)
    def _():
        carry_ref[...] = jnp.zeros_like(carry_ref)

    x = res_ref[...] + _dot(a_ref[...], wo_ref[...])
    x_ref[...] = x
    h = _rms(x, g_ref[...])
    _rows_to_slabs(h_ref, h)
    logits = _dot_nt(rt_ref[...], h.astype(BF16))
    eidx = lax.broadcasted_iota(jnp.int32, logits.shape, 0)
    m1 = jnp.max(logits, axis=0, keepdims=True)
    i1 = jnp.min(jnp.where(logits == m1, eidx, ne), axis=0, keepdims=True)
    rest = jnp.where(eidx == i1, -jnp.inf, logits)
    m2 = jnp.max(rest, axis=0, keepdims=True)
    i2 = jnp.min(jnp.where(rest == m2, eidx, ne), axis=0, keepdims=True)
    e2 = jnp.exp(m2 - m1)
    den = 1.0 + e2
    idx_ref[...] = jnp.concatenate([i1, i2], axis=0)
    gate_ref[...] = jnp.concatenate([1.0 / den, e2 / den], axis=0)

    sel1 = eidx == i1
    sel2 = eidx == i2
    member = jnp.where(sel1 | sel2, 1.0, 0.0)
    before = (lax.broadcasted_iota(jnp.int32, (tm, tm), 0) < lax.broadcasted_iota(jnp.int32, (tm, tm), 1))
    excl = _dot(member.astype(BF16), jnp.where(before, 1.0, 0.0).astype(BF16))
    excl = excl + carry_ref[:, 0:1]
    r1 = jnp.sum(jnp.where(sel1, excl, 0.0), axis=0, keepdims=True)
    r2 = jnp.sum(jnp.where(sel2, excl, 0.0), axis=0, keepdims=True)
    rank_ref[...] = jnp.concatenate([r1, r2], axis=0).astype(jnp.int32)
    total = carry_ref[:, 0:1] + jnp.sum(member, axis=1, keepdims=True)
    carry_ref[...] = jnp.broadcast_to(total, carry_ref.shape)
    cnt_ref[...] = jnp.broadcast_to(total, cnt_ref.shape).astype(jnp.int32)


def _outproj_router(a, w_out, res, gain, router, *, tm):
    n, d = res.shape
    k = a.shape[1]
    ne = router.shape[1]
    return pl.pallas_call(
        _router_kernel,
        grid=(n // tm,),
        in_specs=[
            pl.BlockSpec((tm, k), lambda i: (i, 0)),
            pl.BlockSpec((k, d), lambda i: (0, 0)),
            pl.BlockSpec((tm, d), lambda i: (i, 0)),
            pl.BlockSpec((1, d), lambda i: (0, 0)),
            pl.BlockSpec((ne, d), lambda i: (0, 0)),
        ],
        out_specs=[
            pl.BlockSpec((tm, d), lambda i: (i, 0)),
            pl.BlockSpec((tm * (d // LANES), LANES), lambda i: (i, 0)),
            pl.BlockSpec((TOP_K, tm), lambda i: (0, i)),
            pl.BlockSpec((TOP_K, tm), lambda i: (0, i)),
            pl.BlockSpec((TOP_K, tm), lambda i: (0, i)),
            pl.BlockSpec((ne, 128), lambda i: (0, 0)),
        ],
        out_shape=[
            jax.ShapeDtypeStruct((n, d), F32),
            jax.ShapeDtypeStruct((n * (d // LANES), LANES), F32),
            jax.ShapeDtypeStruct((TOP_K, n), jnp.int32),
            jax.ShapeDtypeStruct((TOP_K, n), F32),
            jax.ShapeDtypeStruct((TOP_K, n), jnp.int32),
            jax.ShapeDtypeStruct((ne, 128), jnp.int32),
        ],
        scratch_shapes=[pltpu.VMEM((ne, 128), F32)],
        compiler_params=pltpu.CompilerParams(dimension_semantics=("arbitrary",)),
        name="router",
    )(a, w_out.astype(BF16), res, gain.reshape(1, d), router.T.astype(BF16))


def _scatter_kernel(pad_start_ref, pad_len_ref, pos_ref, h_ref, xs_ref, zero_ref, sem, zsem, *, tm_exp):
    tm = pos_ref.shape[2]
    dc = h_ref.shape[0] // tm
    ne = pad_start_ref.shape[0] - 1
    bits = [1 << b for b in reversed(range((tm_exp - 1).bit_length()))]

    def zero_copy(first_row, nrows):
        dst = pl.multiple_of(first_row * dc, dc)
        return pltpu.make_async_copy(zero_ref.at[pl.ds(0, nrows * dc)], xs_ref.at[pl.ds(dst, nrows * dc)], zsem)

    def for_each_zero_copy(act):
        for e in range(ne):
            for bit in bits:
                @pl.when((pad_len_ref[e] & bit) != 0)
                def _(e=e, bit=bit):
                    higher = pad_len_ref[e] & ~(2 * bit - 1)
                    act(zero_copy(pad_start_ref[e] + higher, bit))
        for t in range(ne):
            @pl.when(t < pad_len_ref[ne])
            def _(t=t):
                act(zero_copy(pad_start_ref[ne] + t * tm_exp, tm_exp))

    def row_copy(r, slot):
        src = pl.multiple_of(r * dc, dc)
        dst = pl.multiple_of(pos_ref[0, slot, r] * dc, dc)
        return pltpu.make_async_copy(h_ref.at[pl.ds(src, dc)], xs_ref.at[pl.ds(dst, dc)], sem)

    def start(r, carry):
        for slot in range(TOP_K):
            row_copy(r, slot).start(priority=slot)
        return carry

    def wait(r, carry):
        for slot in range(TOP_K):
            row_copy(r, slot).wait()
        return carry

    @pl.when(pl.program_id(0) == 0)
    def _():
        zero_ref[...] = jnp.zeros_like(zero_ref)
        for_each_zero_copy(lambda c: c.start())

    lax.fori_loop(0, tm, start, 0, unroll=8)

    @pl.when(pl.program_id(0) == 0)
    def _():
        for_each_zero_copy(lambda c: c.wait())

    lax.fori_loop(0, tm, wait, 0, unroll=8)


def _scatter_rows(h, pos_tiles, pad_start, pad_len, m_rows, *, tm, tm_exp):
    n = pos_tiles.shape[0] * tm
    dc = h.shape[0] // n
    grid_spec = pltpu.PrefetchScalarGridSpec(
        num_scalar_prefetch=2,
        grid=(n // tm,),
        in_specs=[
            pl.BlockSpec((1, TOP_K, tm), lambda i, ps, pn: (i, 0, 0), memory_space=pltpu.SMEM),
            pl.BlockSpec((tm * dc, LANES), lambda i, ps, pn: (i, 0)),
        ],
        out_specs=pl.BlockSpec(memory_space=pl.ANY),
        scratch_shapes=[pltpu.VMEM((tm_exp * dc, LANES), F32), pltpu.SemaphoreType.DMA(()),
                        pltpu.SemaphoreType.DMA(())],
    )
    return pl.pallas_call(
        functools.partial(_scatter_kernel, tm_exp=tm_exp),
        grid_spec=grid_spec,
        out_shape=jax.ShapeDtypeStruct((m_rows * dc, LANES), F32),
        compiler_params=pltpu.CompilerParams(dimension_semantics=("arbitrary",), has_side_effects=True),
        name="scatter_rows",
    )(pad_start, pad_len, pos_tiles, h)


def _combine_kernel(pos_ref, pos_next_ref, x_ref, gate_ref, g_ref, ys_ref, out_ref, rows_ref, sems):
    tm = x_ref.shape[0]
    i = pl.program_id(0)
    n_steps = pl.num_programs(0)
    buf = i % 2

    dc = rows_ref.shape[2] // tm

    def row_copy(p_ref, b, r, slot):
        src = pl.multiple_of(p_ref[0, slot, r] * dc, dc)
        dst = pl.multiple_of(r * dc, dc)
        return pltpu.make_async_copy(ys_ref.at[pl.ds(src, dc)], rows_ref.at[b, slot, pl.ds(dst, dc)], sems.at[b])

    def start_tile(p_ref, b):
        def body(r, carry):
            for slot in range(TOP_K):
                row_copy(p_ref, b, r, slot).start(priority=slot)
            return carry
        lax.fori_loop(0, tm, body, 0, unroll=8)

    @pl.when(i == 0)
    def _():
        start_tile(pos_ref, 0)

    @pl.when(i + 1 < n_steps)
    def _():
        start_tile(pos_next_ref, 1 - buf)

    def wait(r, carry):
        for slot in range(TOP_K):
            row_copy(pos_ref, buf, r, slot).wait()
        return carry

    lax.fori_loop(0, tm, wait, 0, unroll=8)
    y = (x_ref[...] + gate_ref[:, 0:1] * _slabs_to_rows(rows_ref.at[buf, 0], tm)
         + gate_ref[:, 1:2] * _slabs_to_rows(rows_ref.at[buf, 1], tm))
    out_ref[...] = _rms(y, g_ref[...])


def _combine(x2, pos_tiles, gate_col, gain, ys, *, tm):
    n, d = x2.shape
    dc = d // LANES
    nt = n // tm
    return pl.pallas_call(
        _combine_kernel,
        grid=(nt,),
        in_specs=[
            pl.BlockSpec((1, TOP_K, tm), lambda i: (i, 0, 0), memory_space=pltpu.SMEM),
            pl.BlockSpec((1, TOP_K, tm), lambda i: (jnp.minimum(i + 1, nt - 1), 0, 0), memory_space=pltpu.SMEM),
            pl.BlockSpec((tm, d), lambda i: (i, 0)),
            pl.BlockSpec((tm, TOP_K), lambda i: (i, 0)),
            pl.BlockSpec((1, d), lambda i: (0, 0)),
            pl.BlockSpec(memory_space=pl.ANY),
        ],
        out_specs=pl.BlockSpec((tm, d), lambda i: (i, 0)),
        out_shape=jax.ShapeDtypeStruct((n, d), F32),
        scratch_shapes=[pltpu.VMEM((2, TOP_K, tm * dc, LANES), F32), pltpu.SemaphoreType.DMA((2,))],
        compiler_params=pltpu.CompilerParams(dimension_semantics=("arbitrary",)),
        name="combine",
    )(pos_tiles, pos_tiles, x2, gate_col, gain.reshape(1, d), ys)


def _moe_and_final_norm(a, w_out, res, norm_gain, router, w_gate, w_up, w_down, final_gain,
                        *, tm_route, tm_exp, tf):
    n, d = res.shape
    ne = router.shape[1]
    x2, h, idx_t, gate_t, rank_t, counts = _outproj_router(a, w_out, res, norm_gain, router, tm=tm_route)

    cnt = counts[:, 0]
    padded = ((cnt + tm_exp - 1) // tm_exp) * tm_exp
    ends = jnp.cumsum(padded)
    starts = ends - padded
    pos = rank_t
    for e in range(ne):
        pos = pos + jnp.where(idx_t == e, starts[e], 0)
    pos_tiles = pos.reshape(TOP_K, n // tm_route, tm_route).transpose(1, 0, 2)
    m_rows = TOP_K * n + ne * tm_exp
    nt = m_rows // tm_exp
    n_valid = (ends[-1] // tm_exp).astype(jnp.int32)
    tile_blk = jnp.minimum(jnp.arange(nt, dtype=jnp.int32), n_valid - 1)
    tile_exp = jnp.minimum(
        jnp.sum(ends[None, :] <= (tile_blk * tm_exp)[:, None], axis=1).astype(jnp.int32), ne - 1)

    pad_start = jnp.concatenate([starts + cnt, ends[-1:]]).astype(jnp.int32)
    pad_len = jnp.concatenate([padded - cnt, (nt - n_valid).reshape(1)]).astype(jnp.int32)
    xs = _scatter_rows(h, pos_tiles, pad_start, pad_len, m_rows, tm=tm_route, tm_exp=tm_exp)
    ys = _experts(xs, tile_blk, tile_exp, n_valid.reshape(1), w_gate, w_up, w_down, tm=tm_exp, tf=tf)
    return _combine(x2, pos_tiles, gate_t.T, final_gain, ys, tm=tm_route)


def _trunk(x, l0_norm_mix, l0_ml_w_in, l0_ml_b_if, l0_ml_head_gain, l0_ml_w_out,
           l0_norm_ffn, l0_ffn_w_gate, l0_ffn_w_up, l0_ffn_w_down,
           l1_norm_mix, l1_gdn_w_in, l1_gdn_conv, l1_gdn_a_log, l1_gdn_dt_bias,
           l1_gdn_out_gain, l1_gdn_w_out, l1_norm_ffn, l1_moe_router,
           l1_moe_w_gate, l1_moe_w_up, l1_moe_w_down, final_norm, *, tiles):
    bsz, s, d = x.shape
    n = bsz * s
    nc = s // CHUNK
    x2 = x.reshape(n, d)

    qk_w = ML_HEADS * ML_DQK
    v_w = ML_HEADS * ML_DV
    fm = 2 * qk_w + 2 * v_w
    ng = 2 * ML_HEADS
    col_scale = jnp.concatenate([jnp.ones((qk_w,), F32), jnp.full((qk_w,), ML_DQK ** -0.5, F32),
                                 jnp.ones((2 * v_w + ng,), F32)])
    ne, _, f_exp = l1_moe_w_gate.shape
    p, gcol, grow, moe_gate16 = _inproj(
        x2, l0_norm_mix, (l0_ml_w_in * col_scale).astype(BF16), fm, l0_ml_b_if, jnp.zeros((ng,), F32),
        _mlstm_gate_act, tm=tiles["tm_proj"], tn=tiles["tn_proj"], casts=(l1_moe_w_gate.reshape(ne * d, f_exp),))
    grow = grow.reshape(ng, bsz, nc, CHUNK).transpose(1, 2, 0, 3)
    hs, moe_down16 = _mlstm(p.reshape(bsz, s, fm), gcol.reshape(bsz, s, ng), grow, l0_ml_head_gain, bsz, s,
                            l1_moe_w_down.reshape(ne * f_exp, d), bb=tiles["bb_ml"])

    x2, hn = _outproj_ffn(hs.reshape(n, v_w), l0_ml_w_out, x2, l0_norm_ffn, l0_ffn_w_gate, l0_ffn_w_up,
                          l0_ffn_w_down, l1_norm_mix, tm=tiles["tm_ffn"], tf=tiles["tf"])

    kw = GD_HEADS * GD_DK
    vw = GD_HEADS * GD_DV
    fm = 2 * kw + 2 * vw
    ng = 2 * GD_HEADS
    zeros_h = jnp.zeros((GD_HEADS,), F32)
    p, gcol, grow, moe_up16 = _inproj(
        hn, l1_norm_mix, l1_gdn_w_in.astype(BF16), fm, jnp.concatenate([zeros_h, l1_gdn_dt_bias]),
        jnp.concatenate([zeros_h, l1_gdn_a_log]), _gdn_gate_act, tm=tiles["tm_proj"], tn=tiles["tn_proj"],
        prenormed=True, casts=(l1_moe_w_up.reshape(ne * d, f_exp),))
    grow = _gdn_pair_rows(grow, bsz, nc)
    o = _gdn(p.reshape(bsz, s, fm), gcol.reshape(bsz, s, ng), grow, l1_gdn_conv, l1_gdn_out_gain, bsz, s,
             bb=tiles["bb_gd"])

    out = _moe_and_final_norm(o.reshape(n, vw), l1_gdn_w_out, x2, l1_norm_ffn, l1_moe_router,
                              moe_gate16.reshape(ne, d, f_exp), moe_up16.reshape(ne, d, f_exp),
                              moe_down16.reshape(ne, f_exp, d), final_norm,
                              tm_route=tiles["tm_route"], tm_exp=tiles["tm_exp"], tf=tiles["tf_exp"])
    return out.reshape(bsz, s, d)


def _tiles_for(n, bsz, f):
    cap = lambda t: min(t, n)
    return dict(tm_proj=cap(512), tn_proj=1024, tm_ffn=cap(512), tf=f // 2,
                tm_route=cap(512), tm_exp=cap(256), tf_exp=f // 2, bb_ml=min(4, bsz), bb_gd=min(4, bsz))


def kernel(x, l0_norm_mix, l0_ml_w_in, l0_ml_b_if, l0_ml_head_gain, l0_ml_w_out, l0_norm_ffn, l0_ffn_w_gate, l0_ffn_w_up, l0_ffn_w_down, l1_norm_mix, l1_gdn_w_in, l1_gdn_conv, l1_gdn_a_log, l1_gdn_dt_bias, l1_gdn_out_gain, l1_gdn_w_out, l1_norm_ffn, l1_moe_router, l1_moe_w_gate, l1_moe_w_up, l1_moe_w_down, final_norm):
    n = x.shape[0] * x.shape[1]
    return _trunk(x, l0_norm_mix, l0_ml_w_in, l0_ml_b_if, l0_ml_head_gain, l0_ml_w_out,
                  l0_norm_ffn, l0_ffn_w_gate, l0_ffn_w_up, l0_ffn_w_down,
                  l1_norm_mix, l1_gdn_w_in, l1_gdn_conv, l1_gdn_a_log, l1_gdn_dt_bias,
                  l1_gdn_out_gain, l1_gdn_w_out, l1_norm_ffn, l1_moe_router,
                  l1_moe_w_gate, l1_moe_w_up, l1_moe_w_down, final_norm, tiles=_tiles_for(n, x.shape[0], l0_ffn_w_gate.shape[1]))
```

```python
import functools

import jax
import jax.numpy as jnp
from jax import lax
from jax.experimental import pallas as pl
from jax.experimental.pallas import tpu as pltpu

EPS = 1e-6
CHUNK = 64

ML_HEADS, ML_DQK, ML_DV = 4, 128, 256
GD_HEADS, GD_DK, GD_DV = 8, 128, 128
GD_CONV_K = 4
N_EXPERTS, TOP_K = 8, 2

F32 = jnp.float32
BF16 = jnp.bfloat16

_NT = (((1,), (1,)), ((), ()))
_TN = (((0,), (0,)), ((), ()))


def _dot(a, b):
    return jnp.dot(a, b, preferred_element_type=F32)


def _dot_nt(a, b):
    return lax.dot_general(a, b, _NT, preferred_element_type=F32)


def _dot_tn(a, b):
    return lax.dot_general(a, b, _TN, preferred_element_type=F32)


def _rms(x, g):
    return x * lax.rsqrt(jnp.mean(x * x, axis=-1, keepdims=True) + EPS) * g


def _softplus(x):
    return jnp.maximum(x, 0.0) + jnp.log(1.0 + jnp.exp(-jnp.abs(x)))


def _sigmoid(x):
    return 1.0 / (1.0 + jnp.exp(-x))


def _silu(x):
    return x * _sigmoid(x)


LANES = 128


def _rows_to_slabs(ref, x):
    rows, d = x.shape
    dc = d // LANES
    for c in range(dc):
        ref[pl.ds(c, rows, stride=dc), :] = x[:, c * LANES:(c + 1) * LANES]


def _slabs_to_rows(ref, rows):
    dc = ref.shape[0] // rows
    return jnp.concatenate([ref[pl.ds(c, rows, stride=dc), :] for c in range(dc)], axis=1)


def _mlstm_gate_act(pre, gidx, p1, p2):
    del p2
    z = pre + p1
    return jnp.where(gidx < ML_HEADS, z, -_softplus(-z))


def _gdn_gate_act(pre, gidx, p1, p2):
    beta = _sigmoid(pre)
    g = -(jnp.exp(p2) * _softplus(pre + p1))
    return jnp.where(gidx < GD_HEADS, beta, g)


def _cast_slices(src_refs, dst_refs):
    for src, dst in zip(src_refs, dst_refs):
        dst[...] = src[...].astype(dst.dtype)


def _cast_specs(arrays, steps, index_map):
    specs = [pl.BlockSpec((a.shape[0] // steps, a.shape[1]), index_map) for a in arrays]
    shapes = [jax.ShapeDtypeStruct(a.shape, BF16) for a in arrays]
    return specs, shapes


def _inproj_kernel(*refs, act, tn, n_cast):
    x_ref, g_ref, w_ref, pc_ref, pr_ref = refs[:5]
    cast_src = refs[5:5 + n_cast]
    main_ref, gcol_ref, grow_ref = refs[5 + n_cast:8 + n_cast]
    cast_dst = refs[8 + n_cast:]
    fm = main_ref.shape[1]
    ng = gcol_ref.shape[1]
    h = _rms(x_ref[...], g_ref[...]).astype(BF16)
    for c in range(0, fm, tn):
        main_ref[:, c:c + tn] = _dot(h, w_ref[:, c:c + tn]).astype(main_ref.dtype)
    pre = _dot(h, w_ref[:, fm:fm + LANES])
    gi_c = lax.broadcasted_iota(jnp.int32, pre.shape, 1)
    gcol_ref[...] = act(pre, gi_c, pc_ref[0:1, :], pc_ref[1:2, :])[:, 0:ng]
    pre_t = pre.T[0:ng, :]
    gi_r = lax.broadcasted_iota(jnp.int32, pre_t.shape, 0)
    rows = act(pre_t, gi_r, pr_ref[:, 0:1], pr_ref[:, 1:2])
    for c in range(grow_ref.shape[0]):
        grow_ref[c] = rows[:, c * CHUNK:(c + 1) * CHUNK]
    _cast_slices(cast_src, cast_dst)


def _inproj(x2, gain, w_all, fm, ng, p1, p2, act, *, tm, tn, casts=()):
    n, d = x2.shape
    pc = jnp.zeros((2, LANES), F32).at[:, :ng].set(jnp.stack([p1, p2]).astype(F32))
    pr = pc[:, :ng].T
    steps = n // tm
    cast_specs, cast_shapes = _cast_specs(casts, steps, lambda i: (i, 0))
    return pl.pallas_call(
        functools.partial(_inproj_kernel, act=act, tn=tn, n_cast=len(casts)),
        grid=(steps,),
        in_specs=[
            pl.BlockSpec((tm, d), lambda i: (i, 0)),
            pl.BlockSpec((1, d), lambda i: (0, 0)),
            pl.BlockSpec((d, fm + LANES), lambda i: (0, 0)),
            pl.BlockSpec((2, LANES), lambda i: (0, 0)),
            pl.BlockSpec((ng, 2), lambda i: (0, 0)),
        ] + cast_specs,
        out_specs=[
            pl.BlockSpec((tm, fm), lambda i: (i, 0)),
            pl.BlockSpec((tm, ng), lambda i: (i, 0)),
            pl.BlockSpec((tm // CHUNK, ng, CHUNK), lambda i: (i, 0, 0)),
        ] + cast_specs,
        out_shape=[
            jax.ShapeDtypeStruct((n, fm), BF16),
            jax.ShapeDtypeStruct((n, ng), F32),
            jax.ShapeDtypeStruct((n // CHUNK, ng, CHUNK), F32),
        ] + cast_shapes,
        compiler_params=pltpu.CompilerParams(dimension_semantics=("parallel",)),
        name="inproj",
    )(x2, gain.reshape(1, d), w_all, pc, pr, *casts)


def _cumsum_col_row(v_col, v_row, incl_lower, incl_upper):
    c_col = jnp.sum(jnp.where(incl_lower, v_row, 0.0), axis=1, keepdims=True)
    c_row = jnp.sum(jnp.where(incl_upper, v_col, 0.0), axis=0, keepdims=True)
    return c_col, c_row


def _mlstm_kernel(q_ref, k_ref, v_ref, o_ref, gcol_ref, grow_ref, gain_ref, cast_src, out_ref, cast_dst,
                  c_ref, n_ref, m_ref):
    L = CHUNK
    H = ML_HEADS
    _cast_slices([cast_src], [cast_dst])

    @pl.when(pl.program_id(1) == 0)
    def _():
        c_ref[...] = jnp.zeros_like(c_ref)
        n_ref[...] = jnp.zeros_like(n_ref)
        m_ref[...] = jnp.zeros_like(m_ref)

    row = lax.broadcasted_iota(jnp.int32, (L, L), 0)
    col = lax.broadcasted_iota(jnp.int32, (L, L), 1)
    lower = col <= row
    upper = row <= col
    units = [(bi, h) for bi in range(q_ref.shape[0]) for h in range(H)]
    hs = range(len(units))
    q = [q_ref[bi, :, h * ML_DQK:(h + 1) * ML_DQK] for bi, h in units]
    k = [k_ref[bi, :, h * ML_DQK:(h + 1) * ML_DQK] for bi, h in units]
    v = [v_ref[bi, :, h * ML_DV:(h + 1) * ML_DV] for bi, h in units]
    i_col = [gcol_ref[bi, :, h:h + 1] for bi, h in units]
    i_row = [grow_ref[bi, 0, h:h + 1, :] for bi, h in units]
    b = [_cumsum_col_row(gcol_ref[bi, :, H + h:H + h + 1], grow_ref[bi, 0, H + h:H + h + 1, :], lower, upper)
         for bi, h in units]
    b_col = [c for c, _ in b]
    b_row = [r for _, r in b]
    m_prev = [m_ref[h, 0:1, 0:1] for h in hs]
    c_prev = [c_ref[h] for h in hs]
    n_prev = [n_ref[h, 0:1, :] for h in hs]

    qk = [_dot_nt(q[h], k[h]) for h in hs]
    qc = [_dot(q[h], c_prev[h].astype(BF16)) for h in hs]
    d = [jnp.where(lower, b_col[h] - b_row[h] + i_row[h], -jnp.inf) for h in hs]
    m_inter = [b_col[h] + m_prev[h] for h in hs]
    m_t = [jnp.maximum(jnp.max(d[h], axis=1, keepdims=True), m_inter[h]) for h in hs]
    sc = [qk[h] * jnp.exp(d[h] - m_t[h]) for h in hs]
    a = [jnp.exp(m_inter[h] - m_t[h]) for h in hs]
    num = [_dot(sc[h].astype(BF16), v[h]) + a[h] * qc[h] for h in hs]
    den = [jnp.sum(sc[h], axis=1, keepdims=True)
           + a[h] * jnp.sum(q[h].astype(F32) * n_prev[h], axis=1, keepdims=True) for h in hs]
    hc = [num[h] / jnp.maximum(jnp.abs(den[h]), jnp.exp(-m_t[h])) for h in hs]

    b_last = [c[L - 1:L, :] for c in b_col]
    m_new = [jnp.maximum(b_last[h] + m_prev[h],
                         jnp.max(b_last[h] - b_row[h] + i_row[h], axis=1, keepdims=True)) for h in hs]
    ws_col = [jnp.exp(b_last[h] - b_col[h] + i_col[h] - m_new[h]) for h in hs]
    decay = [jnp.exp(b_last[h] + m_prev[h] - m_new[h]) for h in hs]
    wv = [(ws_col[h] * v[h].astype(F32)).astype(BF16) for h in hs]
    kv = [_dot_tn(k[h], wv[h]) for h in hs]
    for h in hs:
        c_ref[h] = decay[h] * c_prev[h] + kv[h]
        n_new = decay[h] * n_prev[h] + jnp.sum(ws_col[h] * k[h].astype(F32), axis=0, keepdims=True)
        n_ref[h] = jnp.broadcast_to(n_new, n_ref.shape[1:])
        m_ref[h] = jnp.broadcast_to(m_new[h], m_ref.shape[1:])
    for u, (bi, h) in enumerate(units):
        og = o_ref[bi, :, h * ML_DV:(h + 1) * ML_DV].astype(F32)
        gain = gain_ref[0:1, h * ML_DV:(h + 1) * ML_DV]
        y = _rms(hc[u], gain) * _sigmoid(og)
        out_ref[bi, :, h * ML_DV:(h + 1) * ML_DV] = y.astype(out_ref.dtype)


def _mlstm(p, gcol, grow, head_gain, bsz, s, cast, *, bb):
    nc = s // CHUNK
    qk_w = ML_HEADS * ML_DQK
    v_w = ML_HEADS * ML_DV
    ng = 2 * ML_HEADS
    (cast_spec,), (cast_shape,) = _cast_specs([cast], (bsz // bb) * nc, lambda b, c: (b * nc + c, 0))
    return pl.pallas_call(
        _mlstm_kernel,
        grid=(bsz // bb, nc),
        in_specs=[
            pl.BlockSpec((bb, CHUNK, qk_w), lambda b, c: (b, c, 0)),
            pl.BlockSpec((bb, CHUNK, qk_w), lambda b, c: (b, c, 1)),
            pl.BlockSpec((bb, CHUNK, v_w), lambda b, c: (b, c, 1)),
            pl.BlockSpec((bb, CHUNK, v_w), lambda b, c: (b, c, 2)),
            pl.BlockSpec((bb, CHUNK, ng), lambda b, c: (b, c, 0)),
            pl.BlockSpec((bb, 1, ng, CHUNK), lambda b, c: (b, c, 0, 0)),
            pl.BlockSpec((1, v_w), lambda b, c: (0, 0)),
            cast_spec,
        ],
        out_specs=[pl.BlockSpec((bb, CHUNK, v_w), lambda b, c: (b, c, 0)), cast_spec],
        out_shape=[jax.ShapeDtypeStruct((bsz, s, v_w), BF16), cast_shape],
        scratch_shapes=[
            pltpu.VMEM((bb * ML_HEADS, ML_DQK, ML_DV), F32),
            pltpu.VMEM((bb * ML_HEADS, 8, ML_DQK), F32),
            pltpu.VMEM((bb * ML_HEADS, 8, 128), F32),
        ],
        compiler_params=pltpu.CompilerParams(dimension_semantics=("parallel", "arbitrary")),
        name="mlstm",
    )(p, p, p, p, gcol, grow, head_gain.reshape(1, v_w).astype(F32), cast)


def _unit_lower_inverse_all(a_list):
    L = a_list[0].shape[0]
    eye = (lax.broadcasted_iota(jnp.int32, (L, L), 0) == lax.broadcasted_iota(jnp.int32, (L, L), 1)).astype(F32)
    xs = [-a for a in a_list]
    ps = [eye + x for x in xs]
    steps = max(1, (L - 1).bit_length()) - 1
    for _ in range(steps):
        x16 = [x.astype(BF16) for x in xs]
        xs = [_dot(xb, xb) for xb in x16]
        x16 = [x.astype(BF16) for x in xs]
        ps = [p + _dot(p.astype(BF16), xb) for p, xb in zip(ps, x16)]
    return ps


def _gdn_kernel(q_ref, k_ref, v_ref, z_ref, gcol_ref, grow_ref, cw_ref, gain_ref, out_ref,
                s_ref, halo_ref):
    L = CHUNK
    H = GD_HEADS
    kw = H * GD_DK

    @pl.when(pl.program_id(1) == 0)
    def _():
        s_ref[...] = jnp.zeros_like(s_ref)
        halo_ref[...] = jnp.zeros_like(halo_ref)

    halo_rows = halo_ref.shape[2]
    nsh = GD_CONV_K - 1
    rr = lax.broadcasted_iota(jnp.int32, (nsh * L, halo_rows + L), 0)
    cc = lax.broadcasted_iota(jnp.int32, (nsh * L, halo_rows + L), 1)
    shifts = jnp.where(cc == halo_rows + (rr % L) - (rr // L + 1), 1.0, 0.0).astype(BF16)

    def conv_silu(x_ref, part, bi):
        xb = x_ref[bi]
        width = xb.shape[1]
        ext = jnp.concatenate([halo_ref[bi, part, :, 0:width], xb], axis=0)
        r = _dot(shifts, ext)
        cw = cw_ref[:, part * kw:part * kw + width]
        acc = xb.astype(F32) * cw[GD_CONV_K - 1:GD_CONV_K, :]
        for sft in range(1, GD_CONV_K):
            acc = acc + r[(sft - 1) * L:sft * L, :] * cw[GD_CONV_K - 1 - sft:GD_CONV_K - sft, :]
        halo_ref[bi, part, :, 0:width] = xb[L - halo_rows:L, :]
        return _silu(acc)

    bb = q_ref.shape[0]
    qa = [conv_silu(q_ref, 0, bi) for bi in range(bb)]
    ka = [conv_silu(k_ref, 1, bi) for bi in range(bb)]
    va = [conv_silu(v_ref, 2, bi) for bi in range(bb)]

    row = lax.broadcasted_iota(jnp.int32, (L, L), 0)
    col = lax.broadcasted_iota(jnp.int32, (L, L), 1)
    lower = col <= row
    upper = row <= col
    strict = col < row
    units = [(bi, h) for bi in range(bb) for h in range(H)]
    hs = range(len(units))
    beta = [gcol_ref[bi, :, h:h + 1] for bi, h in units]
    gc = [_cumsum_col_row(gcol_ref[bi, :, H + h:H + h + 1], grow_ref[bi, 0, H + h:H + h + 1, :], lower, upper)
          for bi, h in units]
    gc_col = [c for c, _ in gc]
    gamma = [jnp.where(lower, jnp.exp(jnp.where(lower, c - r, 0.0)), 0.0) for c, r in gc]
    eg = [jnp.exp(c) for c in gc_col]
    g_last = [c[L - 1:L, :] for c in gc_col]

    q = [qa[bi][:, h * GD_DK:(h + 1) * GD_DK] for bi, h in units]
    k = [ka[bi][:, h * GD_DK:(h + 1) * GD_DK] for bi, h in units]
    v = [va[bi][:, h * GD_DV:(h + 1) * GD_DV] for bi, h in units]
    q = [x * lax.rsqrt(jnp.sum(x * x, axis=1, keepdims=True) + EPS) * (GD_DK ** -0.5) for x in q]
    k = [x * lax.rsqrt(jnp.sum(x * x, axis=1, keepdims=True) + EPS) for x in k]
    kb = [k[h] * beta[h] for h in hs]
    k16 = [x.astype(BF16) for x in k]

    kq = [_dot_nt(jnp.concatenate([kb[h], q[h]], axis=0).astype(BF16), k16[h]) for h in hs]
    a = [jnp.where(strict, kq[h][0:L] * gamma[h], 0.0) for h in hs]
    attn = [(kq[h][L:2 * L] * gamma[h]).astype(BF16) for h in hs]
    t = [x.astype(BF16) for x in _unit_lower_inverse_all(a)]
    uw = [_dot(t[h], jnp.concatenate([v[h] * beta[h], kb[h] * eg[h]], axis=1).astype(BF16)) for h in hs]

    s_prev = [s_ref[h] for h in hs]
    s16 = [x.astype(BF16) for x in s_prev]
    ws = [_dot(jnp.concatenate([uw[h][:, GD_DV:], q[h] * eg[h]], axis=0).astype(BF16), s16[h]) for h in hs]
    vn16 = [(uw[h][:, 0:GD_DV] - ws[h][0:L]).astype(BF16) for h in hs]
    o = [ws[h][L:2 * L] + _dot(attn[h], vn16[h]) for h in hs]
    kdec = [(k[h] * jnp.exp(g_last[h] - gc_col[h])).astype(BF16) for h in hs]
    for h in hs:
        s_ref[h] = jnp.exp(g_last[h]) * s_prev[h] + _dot_tn(kdec[h], vn16[h])
    for u, (bi, h) in enumerate(units):
        z = z_ref[bi, :, h * GD_DV:(h + 1) * GD_DV].astype(F32)
        y = _rms(o[u], gain_ref[...]) * _silu(z)
        out_ref[bi, :, h * GD_DV:(h + 1) * GD_DV] = y.astype(out_ref.dtype)


def _gdn(p, gcol, grow, conv_w, out_gain, bsz, s, *, bb):
    nc = s // CHUNK
    kw = GD_HEADS * GD_DK
    vw = GD_HEADS * GD_DV
    ng = 2 * GD_HEADS
    cc = conv_w.shape[1]
    return pl.pallas_call(
        _gdn_kernel,
        grid=(bsz // bb, nc),
        in_specs=[
            pl.BlockSpec((bb, CHUNK, kw), lambda b, c: (b, c, 0)),
            pl.BlockSpec((bb, CHUNK, kw), lambda b, c: (b, c, 1)),
            pl.BlockSpec((bb, CHUNK, vw), lambda b, c: (b, c, 2)),
            pl.BlockSpec((bb, CHUNK, vw), lambda b, c: (b, c, 3)),
            pl.BlockSpec((bb, CHUNK, ng), lambda b, c: (b, c, 0)),
            pl.BlockSpec((bb, 1, ng, CHUNK), lambda b, c: (b, c, 0, 0)),
            pl.BlockSpec((GD_CONV_K, cc), lambda b, c: (0, 0)),
            pl.BlockSpec((1, GD_DV), lambda b, c: (0, 0)),
        ],
        out_specs=pl.BlockSpec((bb, CHUNK, vw), lambda b, c: (b, c, 0)),
        out_shape=jax.ShapeDtypeStruct((bsz, s, vw), BF16),
        scratch_shapes=[
            pltpu.VMEM((bb * GD_HEADS, GD_DK, GD_DV), F32),
            pltpu.VMEM((bb, 3, 16, kw), BF16),
        ],
        compiler_params=pltpu.CompilerParams(dimension_semantics=("parallel", "arbitrary")),
        name="gdn",
    )(p, p, p, p, gcol, grow, conv_w.astype(F32), out_gain.reshape(1, GD_DV).astype(F32))


def _ffn_kernel(a_ref, wo_ref, res_ref, g_ref, wg_ref, wu_ref, wd_ref, out_ref, *, tf):
    x1 = res_ref[...] + _dot(a_ref[...], wo_ref[...])
    h = _rms(x1, g_ref[...]).astype(BF16)
    y = x1
    for c in range(0, wg_ref.shape[1], tf):
        a = _silu(_dot(h, wg_ref[:, c:c + tf])) * _dot(h, wu_ref[:, c:c + tf])
        y = y + _dot(a.astype(BF16), wd_ref[c:c + tf, :])
    out_ref[...] = y


def _outproj_ffn(a, w_out, res, gain, w_gate, w_up, w_down, *, tm, tf):
    n, d = res.shape
    k = a.shape[1]
    f = w_gate.shape[1]
    once = dict(pipeline_mode=pl.Buffered(1))
    return pl.pallas_call(
        functools.partial(_ffn_kernel, tf=tf),
        grid=(n // tm,),
        in_specs=[
            pl.BlockSpec((tm, k), lambda i: (i, 0)),
            pl.BlockSpec((k, d), lambda i: (0, 0), **once),
            pl.BlockSpec((tm, d), lambda i: (i, 0)),
            pl.BlockSpec((1, d), lambda i: (0, 0)),
            pl.BlockSpec((d, f), lambda i: (0, 0), **once),
            pl.BlockSpec((d, f), lambda i: (0, 0), **once),
            pl.BlockSpec((f, d), lambda i: (0, 0), **once),
        ],
        out_specs=pl.BlockSpec((tm, d), lambda i: (i, 0)),
        out_shape=jax.ShapeDtypeStruct((n, d), F32),
        compiler_params=pltpu.CompilerParams(dimension_semantics=("parallel",)),
        name="ffn",
    )(a, w_out.astype(BF16), res, gain.reshape(1, d), w_gate.astype(BF16), w_up.astype(BF16),
      w_down.astype(BF16))


def _expert_kernel(blk_ref, exp_ref, nvalid_ref, x_ref, wg_ref, wu_ref, wd_ref, out_ref, *, tm, tf):
    del blk_ref, exp_ref
    i = pl.program_id(0)
    f = wg_ref.shape[2]

    @pl.when(i < nvalid_ref[0])
    def _():
        h = _slabs_to_rows(x_ref, tm).astype(BF16)
        y = None
        for c in range(0, f, tf):
            a = _silu(_dot(h, wg_ref[0, :, c:c + tf])) * _dot(h, wu_ref[0, :, c:c + tf])
            part = _dot(a.astype(BF16), wd_ref[0, c:c + tf, :])
            y = part if y is None else y + part
        _rows_to_slabs(out_ref, y)

    @pl.when(i >= nvalid_ref[0])
    def _():
        out_ref[...] = jnp.zeros_like(out_ref)


def _experts(xs, tile_blk, tile_exp, n_valid, w_gate, w_up, w_down, *, tm, tf):
    d = w_gate.shape[1]
    dc = d // LANES
    m = xs.shape[0] // dc
    f = w_gate.shape[2]
    nt = m // tm
    grid_spec = pltpu.PrefetchScalarGridSpec(
        num_scalar_prefetch=3,
        grid=(nt,),
        in_specs=[
            pl.BlockSpec((tm * dc, LANES), lambda i, blk, ex, nv: (blk[i], 0)),
            pl.BlockSpec((1, d, f), lambda i, blk, ex, nv: (ex[i], 0, 0)),
            pl.BlockSpec((1, d, f), lambda i, blk, ex, nv: (ex[i], 0, 0)),
            pl.BlockSpec((1, f, d), lambda i, blk, ex, nv: (ex[i], 0, 0)),
        ],
        out_specs=pl.BlockSpec((tm * dc, LANES), lambda i, blk, ex, nv: (i, 0)),
    )
    return pl.pallas_call(
        functools.partial(_expert_kernel, tm=tm, tf=tf),
        grid_spec=grid_spec,
        out_shape=jax.ShapeDtypeStruct((m * dc, LANES), F32),
        compiler_params=pltpu.CompilerParams(dimension_semantics=("arbitrary",)),
        name="experts",
    )(tile_blk, tile_exp, n_valid, xs, w_gate.astype(BF16), w_up.astype(BF16), w_down.astype(BF16))


def _router_kernel(a_ref, wo_ref, res_ref, g_ref, rt_ref,
                   x_ref, h_ref, idx_ref, gate_ref, rank_ref, cnt_ref, carry_ref):
    tm = res_ref.shape[0]
    ne = rt_ref.shape[0]

    @pl.when(pl.program_id(0) == 0)
    def _():
        carry_ref[...] = jnp.zeros_like(carry_ref)

    x = res_ref[...] + _dot(a_ref[...], wo_ref[...])
    x_ref[...] = x
    h = _rms(x, g_ref[...])
    _rows_to_slabs(h_ref, h)
    logits = _dot_nt(rt_ref[...], h.astype(BF16))
    eidx = lax.broadcasted_iota(jnp.int32, logits.shape, 0)
    m1 = jnp.max(logits, axis=0, keepdims=True)
    i1 = jnp.min(jnp.where(logits == m1, eidx, ne), axis=0, keepdims=True)
    rest = jnp.where(eidx == i1, -jnp.inf, logits)
    m2 = jnp.max(rest, axis=0, keepdims=True)
    i2 = jnp.min(jnp.where(rest == m2, eidx, ne), axis=0, keepdims=True)
    e2 = jnp.exp(m2 - m1)
    den = 1.0 + e2
    idx_ref[...] = jnp.concatenate([i1, i2], axis=0)
    gate_ref[...] = jnp.concatenate([1.0 / den, e2 / den], axis=0)

    sel1 = eidx == i1
    sel2 = eidx == i2
    member = jnp.where(sel1 | sel2, 1.0, 0.0)
    before = (lax.broadcasted_iota(jnp.int32, (tm, tm), 0) < lax.broadcasted_iota(jnp.int32, (tm, tm), 1))
    excl = _dot(member.astype(BF16), jnp.where(before, 1.0, 0.0).astype(BF16))
    excl = excl + carry_ref[:, 0:1]
    r1 = jnp.sum(jnp.where(sel1, excl, 0.0), axis=0, keepdims=True)
    r2 = jnp.sum(jnp.where(sel2, excl, 0.0), axis=0, keepdims=True)
    rank_ref[...] = jnp.concatenate([r1, r2], axis=0).astype(jnp.int32)
    total = carry_ref[:, 0:1] + jnp.sum(member, axis=1, keepdims=True)
    carry_ref[...] = jnp.broadcast_to(total, carry_ref.shape)
    cnt_ref[...] = jnp.broadcast_to(total, cnt_ref.shape).astype(jnp.int32)


def _outproj_router(a, w_out, res, gain, router, *, tm):
    n, d = res.shape
    k = a.shape[1]
    ne = router.shape[1]
    return pl.pallas_call(
        _router_kernel,
        grid=(n // tm,),
        in_specs=[
            pl.BlockSpec((tm, k), lambda i: (i, 0)),
            pl.BlockSpec((k, d), lambda i: (0, 0)),
            pl.BlockSpec((tm, d), lambda i: (i, 0)),
            pl.BlockSpec((1, d), lambda i: (0, 0)),
            pl.BlockSpec((ne, d), lambda i: (0, 0)),
        ],
        out_specs=[
            pl.BlockSpec((tm, d), lambda i: (i, 0)),
            pl.BlockSpec((tm * (d // LANES), LANES), lambda i: (i, 0)),
            pl.BlockSpec((TOP_K, tm), lambda i: (0, i)),
            pl.BlockSpec((TOP_K, tm), lambda i: (0, i)),
            pl.BlockSpec((TOP_K, tm), lambda i: (0, i)),
            pl.BlockSpec((ne, 128), lambda i: (0, 0)),
        ],
        out_shape=[
            jax.ShapeDtypeStruct((n, d), F32),
            jax.ShapeDtypeStruct((n * (d // LANES), LANES), F32),
            jax.ShapeDtypeStruct((TOP_K, n), jnp.int32),
            jax.ShapeDtypeStruct((TOP_K, n), F32),
            jax.ShapeDtypeStruct((TOP_K, n), jnp.int32),
            jax.ShapeDtypeStruct((ne, 128), jnp.int32),
        ],
        scratch_shapes=[pltpu.VMEM((ne, 128), F32)],
        compiler_params=pltpu.CompilerParams(dimension_semantics=("arbitrary",)),
        name="router",
    )(a, w_out.astype(BF16), res, gain.reshape(1, d), router.T.astype(BF16))


def _scatter_kernel(pad_start_ref, pad_len_ref, pos_ref, h_ref, xs_ref, zero_ref, sem, zsem, *, tm_exp):
    tm = pos_ref.shape[2]
    dc = h_ref.shape[0] // tm
    ne = pad_start_ref.shape[0] - 1
    bits = [1 << b for b in reversed(range((tm_exp - 1).bit_length()))]

    def zero_copy(first_row, nrows):
        dst = pl.multiple_of(first_row * dc, dc)
        return pltpu.make_async_copy(zero_ref.at[pl.ds(0, nrows * dc)], xs_ref.at[pl.ds(dst, nrows * dc)], zsem)

    def for_each_zero_copy(act):
        for e in range(ne):
            for bit in bits:
                @pl.when((pad_len_ref[e] & bit) != 0)
                def _(e=e, bit=bit):
                    higher = pad_len_ref[e] & ~(2 * bit - 1)
                    act(zero_copy(pad_start_ref[e] + higher, bit))
        for t in range(ne):
            @pl.when(t < pad_len_ref[ne])
            def _(t=t):
                act(zero_copy(pad_start_ref[ne] + t * tm_exp, tm_exp))

    def row_copy(r, slot):
        src = pl.multiple_of(r * dc, dc)
        dst = pl.multiple_of(pos_ref[0, slot, r] * dc, dc)
        return pltpu.make_async_copy(h_ref.at[pl.ds(src, dc)], xs_ref.at[pl.ds(dst, dc)], sem)

    def start(r, carry):
        for slot in range(TOP_K):
            row_copy(r, slot).start(priority=slot)
        return carry

    def wait(r, carry):
        for slot in range(TOP_K):
            row_copy(r, slot).wait()
        return carry

    @pl.when(pl.program_id(0) == 0)
    def _():
        zero_ref[...] = jnp.zeros_like(zero_ref)
        for_each_zero_copy(lambda c: c.start())

    lax.fori_loop(0, tm, start, 0, unroll=8)

    @pl.when(pl.program_id(0) == 0)
    def _():
        for_each_zero_copy(lambda c: c.wait())

    lax.fori_loop(0, tm, wait, 0, unroll=8)


def _scatter_rows(h, pos_tiles, pad_start, pad_len, m_rows, *, tm, tm_exp):
    n = pos_tiles.shape[0] * tm
    dc = h.shape[0] // n
    grid_spec = pltpu.PrefetchScalarGridSpec(
        num_scalar_prefetch=2,
        grid=(n // tm,),
        in_specs=[
            pl.BlockSpec((1, TOP_K, tm), lambda i, ps, pn: (i, 0, 0), memory_space=pltpu.SMEM),
            pl.BlockSpec((tm * dc, LANES), lambda i, ps, pn: (i, 0)),
        ],
        out_specs=pl.BlockSpec(memory_space=pl.ANY),
        scratch_shapes=[pltpu.VMEM((tm_exp * dc, LANES), F32), pltpu.SemaphoreType.DMA(()),
                        pltpu.SemaphoreType.DMA(())],
    )
    return pl.pallas_call(
        functools.partial(_scatter_kernel, tm_exp=tm_exp),
        grid_spec=grid_spec,
        out_shape=jax.ShapeDtypeStruct((m_rows * dc, LANES), F32),
        compiler_params=pltpu.CompilerParams(dimension_semantics=("arbitrary",), has_side_effects=True),
        name="scatter_rows",
    )(pad_start, pad_len, pos_tiles, h)


def _combine_kernel(pos_ref, pos_next_ref, x_ref, gate_ref, g_ref, ys_ref, out_ref, rows_ref, sems):
    tm = x_ref.shape[0]
    i = pl.program_id(0)
    n_steps = pl.num_programs(0)
    buf = i % 2

    dc = rows_ref.shape[2] // tm

    def row_copy(p_ref, b, r, slot):
        src = pl.multiple_of(p_ref[0, slot, r] * dc, dc)
        dst = pl.multiple_of(r * dc, dc)
        return pltpu.make_async_copy(ys_ref.at[pl.ds(src, dc)], rows_ref.at[b, slot, pl.ds(dst, dc)], sems.at[b])

    def start_tile(p_ref, b):
        def body(r, carry):
            for slot in range(TOP_K):
                row_copy(p_ref, b, r, slot).start(priority=slot)
            return carry
        lax.fori_loop(0, tm, body, 0, unroll=8)

    @pl.when(i == 0)
    def _():
        start_tile(pos_ref, 0)

    @pl.when(i + 1 < n_steps)
    def _():
        start_tile(pos_next_ref, 1 - buf)

    def wait(r, carry):
        for slot in range(TOP_K):
            row_copy(pos_ref, buf, r, slot).wait()
        return carry

    lax.fori_loop(0, tm, wait, 0, unroll=8)
    y = (x_ref[...] + gate_ref[:, 0:1] * _slabs_to_rows(rows_ref.at[buf, 0], tm)
         + gate_ref[:, 1:2] * _slabs_to_rows(rows_ref.at[buf, 1], tm))
    out_ref[...] = _rms(y, g_ref[...])


def _combine(x2, pos_tiles, gate_col, gain, ys, *, tm):
    n, d = x2.shape
    dc = d // LANES
    nt = n // tm
    return pl.pallas_call(
        _combine_kernel,
        grid=(nt,),
        in_specs=[
            pl.BlockSpec((1, TOP_K, tm), lambda i: (i, 0, 0), memory_space=pltpu.SMEM),
            pl.BlockSpec((1, TOP_K, tm), lambda i: (jnp.minimum(i + 1, nt - 1), 0, 0), memory_space=pltpu.SMEM),
            pl.BlockSpec((tm, d), lambda i: (i, 0)),
            pl.BlockSpec((tm, TOP_K), lambda i: (i, 0)),
            pl.BlockSpec((1, d), lambda i: (0, 0)),
            pl.BlockSpec(memory_space=pl.ANY),
        ],
        out_specs=pl.BlockSpec((tm, d), lambda i: (i, 0)),
        out_shape=jax.ShapeDtypeStruct((n, d), F32),
        scratch_shapes=[pltpu.VMEM((2, TOP_K, tm * dc, LANES), F32), pltpu.SemaphoreType.DMA((2,))],
        compiler_params=pltpu.CompilerParams(dimension_semantics=("arbitrary",)),
        name="combine",
    )(pos_tiles, pos_tiles, x2, gate_col, gain.reshape(1, d), ys)


def _moe_and_final_norm(a, w_out, res, norm_gain, router, w_gate, w_up, w_down, final_gain,
                        *, tm_route, tm_exp, tf):
    n, d = res.shape
    ne = router.shape[1]
    x2, h, idx_t, gate_t, rank_t, counts = _outproj_router(a, w_out, res, norm_gain, router, tm=tm_route)

    cnt = counts[:, 0]
    padded = ((cnt + tm_exp - 1) // tm_exp) * tm_exp
    ends = jnp.cumsum(padded)
    starts = ends - padded
    pos = rank_t
    for e in range(ne):
        pos = pos + jnp.where(idx_t == e, starts[e], 0)
    pos_tiles = pos.reshape(TOP_K, n // tm_route, tm_route).transpose(1, 0, 2)
    m_rows = TOP_K * n + ne * tm_exp
    nt = m_rows // tm_exp
    n_valid = (ends[-1] // tm_exp).astype(jnp.int32)
    tile_blk = jnp.minimum(jnp.arange(nt, dtype=jnp.int32), n_valid - 1)
    tile_exp = jnp.minimum(
        jnp.sum(ends[None, :] <= (tile_blk * tm_exp)[:, None], axis=1).astype(jnp.int32), ne - 1)

    pad_start = jnp.concatenate([starts + cnt, ends[-1:]]).astype(jnp.int32)
    pad_len = jnp.concatenate([padded - cnt, (nt - n_valid).reshape(1)]).astype(jnp.int32)
    xs = _scatter_rows(h, pos_tiles, pad_start, pad_len, m_rows, tm=tm_route, tm_exp=tm_exp)
    ys = _experts(xs, tile_blk, tile_exp, n_valid.reshape(1), w_gate, w_up, w_down, tm=tm_exp, tf=tf)
    return _combine(x2, pos_tiles, gate_t.T, final_gain, ys, tm=tm_route)


def _trunk(x, l0_norm_mix, l0_ml_w_in, l0_ml_b_if, l0_ml_head_gain, l0_ml_w_out,
           l0_norm_ffn, l0_ffn_w_gate, l0_ffn_w_up, l0_ffn_w_down,
           l1_norm_mix, l1_gdn_w_in, l1_gdn_conv, l1_gdn_a_log, l1_gdn_dt_bias,
           l1_gdn_out_gain, l1_gdn_w_out, l1_norm_ffn, l1_moe_router,
           l1_moe_w_gate, l1_moe_w_up, l1_moe_w_down, final_norm, *, tiles):
    bsz, s, d = x.shape
    n = bsz * s
    nc = s // CHUNK
    x2 = x.reshape(n, d)

    qk_w = ML_HEADS * ML_DQK
    v_w = ML_HEADS * ML_DV
    fm = 2 * qk_w + 2 * v_w
    ng = 2 * ML_HEADS
    col_scale = jnp.concatenate([jnp.ones((qk_w,), F32), jnp.full((qk_w,), ML_DQK ** -0.5, F32),
                                 jnp.ones((2 * v_w + ng,), F32)])
    ne, _, f_exp = l1_moe_w_gate.shape
    pad_gates = lambda w: jnp.pad(w, ((0, 0), (0, LANES - ng))).astype(BF16)
    p, gcol, grow, moe_gate16 = _inproj(
        x2, l0_norm_mix, pad_gates(l0_ml_w_in * col_scale), fm, ng, l0_ml_b_if, jnp.zeros((ng,), F32),
        _mlstm_gate_act, tm=tiles["tm_proj"], tn=tiles["tn_proj"], casts=(l1_moe_w_gate.reshape(ne * d, f_exp),))
    hs, moe_down16 = _mlstm(p.reshape(bsz, s, fm), gcol.reshape(bsz, s, ng), grow.reshape(bsz, nc, ng, CHUNK),
                            l0_ml_head_gain, bsz, s, l1_moe_w_down.reshape(ne * f_exp, d), bb=tiles["bb_ml"])

    x2 = _outproj_ffn(hs.reshape(n, v_w), l0_ml_w_out, x2, l0_norm_ffn, l0_ffn_w_gate, l0_ffn_w_up,
                      l0_ffn_w_down, tm=tiles["tm_ffn"], tf=tiles["tf"])

    kw = GD_HEADS * GD_DK
    vw = GD_HEADS * GD_DV
    fm = 2 * kw + 2 * vw
    ng = 2 * GD_HEADS
    zeros_h = jnp.zeros((GD_HEADS,), F32)
    pad_gates = lambda w: jnp.pad(w, ((0, 0), (0, LANES - ng))).astype(BF16)
    p, gcol, grow, moe_up16 = _inproj(
        x2, l1_norm_mix, pad_gates(l1_gdn_w_in), fm, ng, jnp.concatenate([zeros_h, l1_gdn_dt_bias]),
        jnp.concatenate([zeros_h, l1_gdn_a_log]), _gdn_gate_act, tm=tiles["tm_proj"], tn=tiles["tn_proj"],
        casts=(l1_moe_w_up.reshape(ne * d, f_exp),))
    o = _gdn(p.reshape(bsz, s, fm), gcol.reshape(bsz, s, ng), grow.reshape(bsz, nc, ng, CHUNK), l1_gdn_conv,
             l1_gdn_out_gain, bsz, s, bb=tiles["bb_gd"])

    out = _moe_and_final_norm(o.reshape(n, vw), l1_gdn_w_out, x2, l1_norm_ffn, l1_moe_router,
                              moe_gate16.reshape(ne, d, f_exp), moe_up16.reshape(ne, d, f_exp),
                              moe_down16.reshape(ne, f_exp, d), final_norm,
                              tm_route=tiles["tm_route"], tm_exp=tiles["tm_exp"], tf=tiles["tf_exp"])
    return out.reshape(bsz, s, d)


def _tiles_for(n, bsz, f):
    cap = lambda t: min(t, n)
    return dict(tm_proj=cap(512), tn_proj=1024, tm_ffn=cap(512), tf=f // 2,
                tm_route=cap(512), tm_exp=cap(256), tf_exp=f // 2, bb_ml=min(4, bsz), bb_gd=min(4, bsz))


def kernel(x, l0_norm_mix, l0_ml_w_in, l0_ml_b_if, l0_ml_head_gain, l0_ml_w_out, l0_norm_ffn, l0_ffn_w_gate, l0_ffn_w_up, l0_ffn_w_down, l1_norm_mix, l1_gdn_w_in, l1_gdn_conv, l1_gdn_a_log, l1_gdn_dt_bias, l1_gdn_out_gain, l1_gdn_w_out, l1_norm_ffn, l1_moe_router, l1_moe_w_gate, l1_moe_w_up, l1_moe_w_down, final_norm):
    n = x.shape[0] * x.shape[1]
    return _trunk(x, l0_norm_mix, l0_ml_w_in, l0_ml_b_if, l0_ml_head_gain, l0_ml_w_out,
                  l0_norm_ffn, l0_ffn_w_gate, l0_ffn_w_up, l0_ffn_w_down,
                  l1_norm_mix, l1_gdn_w_in, l1_gdn_conv, l1_gdn_a_log, l1_gdn_dt_bias,
                  l1_gdn_out_gain, l1_gdn_w_out, l1_norm_ffn, l1_moe_router,
                  l1_moe_w_gate, l1_moe_w_up, l1_moe_w_down, final_norm, tiles=_tiles_for(n, x.shape[0], l0_ffn_w_gate.shape[1]))
```

```python
import functools

import jax
import jax.numpy as jnp
from jax import lax
from jax.experimental import pallas as pl
from jax.experimental.pallas import tpu as pltpu

EPS = 1e-6
CHUNK = 64

ML_HEADS, ML_DQK, ML_DV = 4, 128, 256
GD_HEADS, GD_DK, GD_DV = 8, 128, 128
GD_CONV_K = 4
N_EXPERTS, TOP_K = 8, 2

F32 = jnp.float32
BF16 = jnp.bfloat16

_NT = (((1,), (1,)), ((), ()))
_TN = (((0,), (0,)), ((), ()))


def _dot(a, b):
    return jnp.dot(a, b, preferred_element_type=F32)


def _dot_nt(a, b):
    return lax.dot_general(a, b, _NT, preferred_element_type=F32)


def _dot_tn(a, b):
    return lax.dot_general(a, b, _TN, preferred_element_type=F32)


def _rms(x, g):
    return x * lax.rsqrt(jnp.mean(x * x, axis=-1, keepdims=True) + EPS) * g


def _softplus(x):
    return jnp.maximum(x, 0.0) + jnp.log(1.0 + jnp.exp(-jnp.abs(x)))


def _sigmoid(x):
    return 1.0 / (1.0 + jnp.exp(-x))


def _silu(x):
    return x * _sigmoid(x)


LANES = 128


def _rows_to_slabs(ref, x):
    rows, d = x.shape
    dc = d // LANES
    for c in range(dc):
        ref[pl.ds(c, rows, stride=dc), :] = x[:, c * LANES:(c + 1) * LANES]


def _slabs_to_rows(ref, rows):
    dc = ref.shape[0] // rows
    return jnp.concatenate([ref[pl.ds(c, rows, stride=dc), :] for c in range(dc)], axis=1)


def _mlstm_gate_act(pre, gidx, p1, p2):
    del p2
    z = pre + p1
    return jnp.where(gidx < ML_HEADS, z, -_softplus(-z))


def _gdn_gate_act(pre, gidx, p1, p2):
    beta = _sigmoid(pre)
    g = -(jnp.exp(p2) * _softplus(pre + p1))
    return jnp.where(gidx < GD_HEADS, beta, g)


def _cast_slices(src_refs, dst_refs):
    for src, dst in zip(src_refs, dst_refs):
        dst[...] = src[...].astype(dst.dtype)


def _cast_specs(arrays, steps, index_map):
    specs = [pl.BlockSpec((a.shape[0] // steps, a.shape[1]), index_map) for a in arrays]
    shapes = [jax.ShapeDtypeStruct(a.shape, BF16) for a in arrays]
    return specs, shapes


def _inproj_kernel(*refs, act, tn, n_cast):
    x_ref, g_ref, w_ref, pc_ref, pr_ref = refs[:5]
    cast_src = refs[5:5 + n_cast]
    main_ref, gcol_ref, grow_ref = refs[5 + n_cast:8 + n_cast]
    cast_dst = refs[8 + n_cast:]
    fm = main_ref.shape[1]
    ng = gcol_ref.shape[1]
    h = _rms(x_ref[...], g_ref[...]).astype(BF16)
    for c in range(0, fm, tn):
        main_ref[:, c:c + tn] = _dot(h, w_ref[:, c:c + tn]).astype(main_ref.dtype)
    pre = _dot(h, w_ref[:, fm:fm + LANES])
    gi_c = lax.broadcasted_iota(jnp.int32, pre.shape, 1)
    gcol_ref[...] = act(pre, gi_c, pc_ref[0:1, :], pc_ref[1:2, :])[:, 0:ng]
    pre_t = pre.T[0:ng, :]
    gi_r = lax.broadcasted_iota(jnp.int32, pre_t.shape, 0)
    rows = act(pre_t, gi_r, pr_ref[:, 0:1], pr_ref[:, 1:2])
    for c in range(grow_ref.shape[0]):
        grow_ref[c] = rows[:, c * CHUNK:(c + 1) * CHUNK]
    _cast_slices(cast_src, cast_dst)


def _inproj(x2, gain, w_all, fm, ng, p1, p2, act, *, tm, tn, casts=()):
    n, d = x2.shape
    pc = jnp.zeros((2, LANES), F32).at[:, :ng].set(jnp.stack([p1, p2]).astype(F32))
    pr = pc[:, :ng].T
    steps = n // tm
    cast_specs, cast_shapes = _cast_specs(casts, steps, lambda i: (i, 0))
    return pl.pallas_call(
        functools.partial(_inproj_kernel, act=act, tn=tn, n_cast=len(casts)),
        grid=(steps,),
        in_specs=[
            pl.BlockSpec((tm, d), lambda i: (i, 0)),
            pl.BlockSpec((1, d), lambda i: (0, 0)),
            pl.BlockSpec((d, fm + LANES), lambda i: (0, 0)),
            pl.BlockSpec((2, LANES), lambda i: (0, 0)),
            pl.BlockSpec((ng, 2), lambda i: (0, 0)),
        ] + cast_specs,
        out_specs=[
            pl.BlockSpec((tm, fm), lambda i: (i, 0)),
            pl.BlockSpec((tm, ng), lambda i: (i, 0)),
            pl.BlockSpec((tm // CHUNK, ng, CHUNK), lambda i: (i, 0, 0)),
        ] + cast_specs,
        out_shape=[
            jax.ShapeDtypeStruct((n, fm), BF16),
            jax.ShapeDtypeStruct((n, ng), F32),
            jax.ShapeDtypeStruct((n // CHUNK, ng, CHUNK), F32),
        ] + cast_shapes,
        compiler_params=pltpu.CompilerParams(dimension_semantics=("parallel",)),
        name="inproj",
    )(x2, gain.reshape(1, d), w_all, pc, pr, *casts)


def _cumsum_col_row(v_col, v_row, incl_lower, incl_upper):
    c_col = jnp.sum(jnp.where(incl_lower, v_row, 0.0), axis=1, keepdims=True)
    c_row = jnp.sum(jnp.where(incl_upper, v_col, 0.0), axis=0, keepdims=True)
    return c_col, c_row


def _mlstm_kernel(q_ref, k_ref, v_ref, o_ref, gcol_ref, grow_ref, gain_ref, cast_src, out_ref, cast_dst,
                  c_ref, n_ref, m_ref):
    L = CHUNK
    H = ML_HEADS
    _cast_slices([cast_src], [cast_dst])

    @pl.when(pl.program_id(1) == 0)
    def _():
        c_ref[...] = jnp.zeros_like(c_ref)
        n_ref[...] = jnp.zeros_like(n_ref)
        m_ref[...] = jnp.zeros_like(m_ref)

    row = lax.broadcasted_iota(jnp.int32, (L, L), 0)
    col = lax.broadcasted_iota(jnp.int32, (L, L), 1)
    lower = col <= row
    upper = row <= col
    units = [(bi, h) for bi in range(q_ref.shape[0]) for h in range(H)]
    hs = range(len(units))
    q = [q_ref[bi, :, h * ML_DQK:(h + 1) * ML_DQK] for bi, h in units]
    k = [k_ref[bi, :, h * ML_DQK:(h + 1) * ML_DQK] for bi, h in units]
    v = [v_ref[bi, :, h * ML_DV:(h + 1) * ML_DV] for bi, h in units]
    i_col = [gcol_ref[bi, :, h:h + 1] for bi, h in units]
    i_row = [grow_ref[bi, 0, h:h + 1, :] for bi, h in units]
    b = [_cumsum_col_row(gcol_ref[bi, :, H + h:H + h + 1], grow_ref[bi, 0, H + h:H + h + 1, :], lower, upper)
         for bi, h in units]
    b_col = [c for c, _ in b]
    b_row = [r for _, r in b]
    m_prev = [m_ref[h, 0:1, 0:1] for h in hs]
    c_prev = [c_ref[h] for h in hs]
    n_prev = [n_ref[h, 0:1, :] for h in hs]

    qk = [_dot_nt(q[h], k[h]) for h in hs]
    qc = [_dot(q[h], c_prev[h].astype(BF16)) for h in hs]
    d = [jnp.where(lower, b_col[h] - b_row[h] + i_row[h], -jnp.inf) for h in hs]
    m_inter = [b_col[h] + m_prev[h] for h in hs]
    m_t = [jnp.maximum(jnp.max(d[h], axis=1, keepdims=True), m_inter[h]) for h in hs]
    sc = [qk[h] * jnp.exp(d[h] - m_t[h]) for h in hs]
    a = [jnp.exp(m_inter[h] - m_t[h]) for h in hs]
    num = [_dot(sc[h].astype(BF16), v[h]) + a[h] * qc[h] for h in hs]
    den = [jnp.sum(sc[h], axis=1, keepdims=True)
           + a[h] * jnp.sum(q[h].astype(F32) * n_prev[h], axis=1, keepdims=True) for h in hs]
    hc = [num[h] / jnp.maximum(jnp.abs(den[h]), jnp.exp(-m_t[h])) for h in hs]

    b_last = [c[L - 1:L, :] for c in b_col]
    m_new = [jnp.maximum(b_last[h] + m_prev[h],
                         jnp.max(b_last[h] - b_row[h] + i_row[h], axis=1, keepdims=True)) for h in hs]
    ws_col = [jnp.exp(b_last[h] - b_col[h] + i_col[h] - m_new[h]) for h in hs]
    decay = [jnp.exp(b_last[h] + m_prev[h] - m_new[h]) for h in hs]
    wv = [(ws_col[h] * v[h].astype(F32)).astype(BF16) for h in hs]
    kv = [_dot_tn(k[h], wv[h]) for h in hs]
    for h in hs:
        c_ref[h] = decay[h] * c_prev[h] + kv[h]
        n_new = decay[h] * n_prev[h] + jnp.sum(ws_col[h] * k[h].astype(F32), axis=0, keepdims=True)
        n_ref[h] = jnp.broadcast_to(n_new, n_ref.shape[1:])
        m_ref[h] = jnp.broadcast_to(m_new[h], m_ref.shape[1:])
    for u, (bi, h) in enumerate(units):
        og = o_ref[bi, :, h * ML_DV:(h + 1) * ML_DV].astype(F32)
        gain = gain_ref[0:1, h * ML_DV:(h + 1) * ML_DV]
        y = _rms(hc[u], gain) * _sigmoid(og)
        out_ref[bi, :, h * ML_DV:(h + 1) * ML_DV] = y.astype(out_ref.dtype)


def _mlstm(p, gcol, grow, head_gain, bsz, s, cast, *, bb):
    nc = s // CHUNK
    qk_w = ML_HEADS * ML_DQK
    v_w = ML_HEADS * ML_DV
    ng = 2 * ML_HEADS
    (cast_spec,), (cast_shape,) = _cast_specs([cast], (bsz // bb) * nc, lambda b, c: (b * nc + c, 0))
    return pl.pallas_call(
        _mlstm_kernel,
        grid=(bsz // bb, nc),
        in_specs=[
            pl.BlockSpec((bb, CHUNK, qk_w), lambda b, c: (b, c, 0)),
            pl.BlockSpec((bb, CHUNK, qk_w), lambda b, c: (b, c, 1)),
            pl.BlockSpec((bb, CHUNK, v_w), lambda b, c: (b, c, 1)),
            pl.BlockSpec((bb, CHUNK, v_w), lambda b, c: (b, c, 2)),
            pl.BlockSpec((bb, CHUNK, ng), lambda b, c: (b, c, 0)),
            pl.BlockSpec((bb, 1, ng, CHUNK), lambda b, c: (b, c, 0, 0)),
            pl.BlockSpec((1, v_w), lambda b, c: (0, 0)),
            cast_spec,
        ],
        out_specs=[pl.BlockSpec((bb, CHUNK, v_w), lambda b, c: (b, c, 0)), cast_spec],
        out_shape=[jax.ShapeDtypeStruct((bsz, s, v_w), BF16), cast_shape],
        scratch_shapes=[
            pltpu.VMEM((bb * ML_HEADS, ML_DQK, ML_DV), F32),
            pltpu.VMEM((bb * ML_HEADS, 8, ML_DQK), F32),
            pltpu.VMEM((bb * ML_HEADS, 8, 128), F32),
        ],
        compiler_params=pltpu.CompilerParams(dimension_semantics=("parallel", "arbitrary")),
        name="mlstm",
    )(p, p, p, p, gcol, grow, head_gain.reshape(1, v_w).astype(F32), cast)


def _unit_lower_inverse_all(a_list):
    L = a_list[0].shape[0]
    eye = (lax.broadcasted_iota(jnp.int32, (L, L), 0) == lax.broadcasted_iota(jnp.int32, (L, L), 1)).astype(F32)
    xs = [-a for a in a_list]
    ps = [eye + x for x in xs]
    steps = max(1, (L - 1).bit_length()) - 1
    for _ in range(steps):
        x16 = [x.astype(BF16) for x in xs]
        xs = [_dot(xb, xb) for xb in x16]
        x16 = [x.astype(BF16) for x in xs]
        ps = [p + _dot(p.astype(BF16), xb) for p, xb in zip(ps, x16)]
    return ps


def _gdn_kernel(q_ref, k_ref, v_ref, z_ref, gcol_ref, grow_ref, cw_ref, gain_ref, out_ref,
                s_ref, halo_ref):
    L = CHUNK
    H = GD_HEADS
    kw = H * GD_DK

    @pl.when(pl.program_id(1) == 0)
    def _():
        s_ref[...] = jnp.zeros_like(s_ref)
        halo_ref[...] = jnp.zeros_like(halo_ref)

    halo_rows = halo_ref.shape[2]
    nsh = GD_CONV_K - 1
    rr = lax.broadcasted_iota(jnp.int32, (nsh * L, halo_rows + L), 0)
    cc = lax.broadcasted_iota(jnp.int32, (nsh * L, halo_rows + L), 1)
    shifts = jnp.where(cc == halo_rows + (rr % L) - (rr // L + 1), 1.0, 0.0).astype(BF16)

    def conv_silu(x_ref, part, bi):
        xb = x_ref[bi]
        width = xb.shape[1]
        ext = jnp.concatenate([halo_ref[bi, part, :, 0:width], xb], axis=0)
        r = _dot(shifts, ext)
        cw = cw_ref[:, part * kw:part * kw + width]
        acc = xb.astype(F32) * cw[GD_CONV_K - 1:GD_CONV_K, :]
        for sft in range(1, GD_CONV_K):
            acc = acc + r[(sft - 1) * L:sft * L, :] * cw[GD_CONV_K - 1 - sft:GD_CONV_K - sft, :]
        halo_ref[bi, part, :, 0:width] = xb[L - halo_rows:L, :]
        return _silu(acc)

    bb = q_ref.shape[0]
    qa = [conv_silu(q_ref, 0, bi) for bi in range(bb)]
    ka = [conv_silu(k_ref, 1, bi) for bi in range(bb)]
    va = [conv_silu(v_ref, 2, bi) for bi in range(bb)]

    row = lax.broadcasted_iota(jnp.int32, (L, L), 0)
    col = lax.broadcasted_iota(jnp.int32, (L, L), 1)
    lower = col <= row
    upper = row <= col
    strict = col < row
    units = [(bi, h) for bi in range(bb) for h in range(H)]
    hs = range(len(units))
    beta = [gcol_ref[bi, :, h:h + 1] for bi, h in units]
    gc = [_cumsum_col_row(gcol_ref[bi, :, H + h:H + h + 1], grow_ref[bi, 0, H + h:H + h + 1, :], lower, upper)
          for bi, h in units]
    gc_col = [c for c, _ in gc]
    gamma = [jnp.where(lower, jnp.exp(jnp.where(lower, c - r, 0.0)), 0.0) for c, r in gc]
    eg = [jnp.exp(c) for c in gc_col]
    g_last = [c[L - 1:L, :] for c in gc_col]

    q = [qa[bi][:, h * GD_DK:(h + 1) * GD_DK] for bi, h in units]
    k = [ka[bi][:, h * GD_DK:(h + 1) * GD_DK] for bi, h in units]
    v = [va[bi][:, h * GD_DV:(h + 1) * GD_DV] for bi, h in units]
    q = [x * lax.rsqrt(jnp.sum(x * x, axis=1, keepdims=True) + EPS) * (GD_DK ** -0.5) for x in q]
    k = [x * lax.rsqrt(jnp.sum(x * x, axis=1, keepdims=True) + EPS) for x in k]
    kb = [k[h] * beta[h] for h in hs]
    k16 = [x.astype(BF16) for x in k]

    kq = [_dot_nt(jnp.concatenate([kb[h], q[h]], axis=0).astype(BF16), k16[h]) for h in hs]
    a = [jnp.where(strict, kq[h][0:L] * gamma[h], 0.0) for h in hs]
    attn = [(kq[h][L:2 * L] * gamma[h]).astype(BF16) for h in hs]
    t = [x.astype(BF16) for x in _unit_lower_inverse_all(a)]
    uw = [_dot(t[h], jnp.concatenate([v[h] * beta[h], kb[h] * eg[h]], axis=1).astype(BF16)) for h in hs]

    s_prev = [s_ref[h] for h in hs]
    s16 = [x.astype(BF16) for x in s_prev]
    ws = [_dot(jnp.concatenate([uw[h][:, GD_DV:], q[h] * eg[h]], axis=0).astype(BF16), s16[h]) for h in hs]
    vn16 = [(uw[h][:, 0:GD_DV] - ws[h][0:L]).astype(BF16) for h in hs]
    o = [ws[h][L:2 * L] + _dot(attn[h], vn16[h]) for h in hs]
    kdec = [(k[h] * jnp.exp(g_last[h] - gc_col[h])).astype(BF16) for h in hs]
    for h in hs:
        s_ref[h] = jnp.exp(g_last[h]) * s_prev[h] + _dot_tn(kdec[h], vn16[h])
    for u, (bi, h) in enumerate(units):
        z = z_ref[bi, :, h * GD_DV:(h + 1) * GD_DV].astype(F32)
        y = _rms(o[u], gain_ref[...]) * _silu(z)
        out_ref[bi, :, h * GD_DV:(h + 1) * GD_DV] = y.astype(out_ref.dtype)


def _gdn(p, gcol, grow, conv_w, out_gain, bsz, s, *, bb):
    nc = s // CHUNK
    kw = GD_HEADS * GD_DK
    vw = GD_HEADS * GD_DV
    ng = 2 * GD_HEADS
    cc = conv_w.shape[1]
    return pl.pallas_call(
        _gdn_kernel,
        grid=(bsz // bb, nc),
        in_specs=[
            pl.BlockSpec((bb, CHUNK, kw), lambda b, c: (b, c, 0)),
            pl.BlockSpec((bb, CHUNK, kw), lambda b, c: (b, c, 1)),
            pl.BlockSpec((bb, CHUNK, vw), lambda b, c: (b, c, 2)),
            pl.BlockSpec((bb, CHUNK, vw), lambda b, c: (b, c, 3)),
            pl.BlockSpec((bb, CHUNK, ng), lambda b, c: (b, c, 0)),
            pl.BlockSpec((bb, 1, ng, CHUNK), lambda b, c: (b, c, 0, 0)),
            pl.BlockSpec((GD_CONV_K, cc), lambda b, c: (0, 0)),
            pl.BlockSpec((1, GD_DV), lambda b, c: (0, 0)),
        ],
        out_specs=pl.BlockSpec((bb, CHUNK, vw), lambda b, c: (b, c, 0)),
        out_shape=jax.ShapeDtypeStruct((bsz, s, vw), BF16),
        scratch_shapes=[
            pltpu.VMEM((bb * GD_HEADS, GD_DK, GD_DV), F32),
            pltpu.VMEM((bb, 3, 16, kw), BF16),
        ],
        compiler_params=pltpu.CompilerParams(dimension_semantics=("parallel", "arbitrary")),
        name="gdn",
    )(p, p, p, p, gcol, grow, conv_w.astype(F32), out_gain.reshape(1, GD_DV).astype(F32))


def _ffn_kernel(a_ref, wo_ref, res_ref, g_ref, wg_ref, wu_ref, wd_ref, out_ref, *, tf):
    x1 = res_ref[...] + _dot(a_ref[...], wo_ref[...])
    h = _rms(x1, g_ref[...]).astype(BF16)
    y = x1
    for c in range(0, wg_ref.shape[1], tf):
        a = _silu(_dot(h, wg_ref[:, c:c + tf])) * _dot(h, wu_ref[:, c:c + tf])
        y = y + _dot(a.astype(BF16), wd_ref[c:c + tf, :])
    out_ref[...] = y


def _outproj_ffn(a, w_out, res, gain, w_gate, w_up, w_down, *, tm, tf):
    n, d = res.shape
    k = a.shape[1]
    f = w_gate.shape[1]
    once = dict(pipeline_mode=pl.Buffered(1))
    return pl.pallas_call(
        functools.partial(_ffn_kernel, tf=tf),
        grid=(n // tm,),
        in_specs=[
            pl.BlockSpec((tm, k), lambda i: (i, 0)),
            pl.BlockSpec((k, d), lambda i: (0, 0), **once),
            pl.BlockSpec((tm, d), lambda i: (i, 0)),
            pl.BlockSpec((1, d), lambda i: (0, 0)),
            pl.BlockSpec((d, f), lambda i: (0, 0), **once),
            pl.BlockSpec((d, f), lambda i: (0, 0), **once),
            pl.BlockSpec((f, d), lambda i: (0, 0), **once),
        ],
        out_specs=pl.BlockSpec((tm, d), lambda i: (i, 0)),
        out_shape=jax.ShapeDtypeStruct((n, d), F32),
        compiler_params=pltpu.CompilerParams(dimension_semantics=("parallel",)),
        name="ffn",
    )(a, w_out.astype(BF16), res, gain.reshape(1, d), w_gate.astype(BF16), w_up.astype(BF16),
      w_down.astype(BF16))


def _expert_kernel(blk_ref, exp_ref, nvalid_ref, x_ref, wg_ref, wu_ref, wd_ref, out_ref, *, tm, tf):
    del blk_ref, exp_ref
    i = pl.program_id(0)
    f = wg_ref.shape[2]

    @pl.when(i < nvalid_ref[0])
    def _():
        h = _slabs_to_rows(x_ref, tm).astype(BF16)
        y = None
        for c in range(0, f, tf):
            a = _silu(_dot(h, wg_ref[0, :, c:c + tf])) * _dot(h, wu_ref[0, :, c:c + tf])
            part = _dot(a.astype(BF16), wd_ref[0, c:c + tf, :])
            y = part if y is None else y + part
        _rows_to_slabs(out_ref, y)

    @pl.when(i >= nvalid_ref[0])
    def _():
        out_ref[...] = jnp.zeros_like(out_ref)


def _experts(xs, tile_blk, tile_exp, n_valid, w_gate, w_up, w_down, *, tm, tf):
    d = w_gate.shape[1]
    dc = d // LANES
    m = xs.shape[0] // dc
    f = w_gate.shape[2]
    nt = m // tm
    grid_spec = pltpu.PrefetchScalarGridSpec(
        num_scalar_prefetch=3,
        grid=(nt,),
        in_specs=[
            pl.BlockSpec((tm * dc, LANES), lambda i, blk, ex, nv: (blk[i], 0)),
            pl.BlockSpec((1, d, f), lambda i, blk, ex, nv: (ex[i], 0, 0)),
            pl.BlockSpec((1, d, f), lambda i, blk, ex, nv: (ex[i], 0, 0)),
            pl.BlockSpec((1, f, d), lambda i, blk, ex, nv: (ex[i], 0, 0)),
        ],
        out_specs=pl.BlockSpec((tm * dc, LANES), lambda i, blk, ex, nv: (i, 0)),
    )
    return pl.pallas_call(
        functools.partial(_expert_kernel, tm=tm, tf=tf),
        grid_spec=grid_spec,
        out_shape=jax.ShapeDtypeStruct((m * dc, LANES), F32),
        compiler_params=pltpu.CompilerParams(dimension_semantics=("arbitrary",)),
        name="experts",
    )(tile_blk, tile_exp, n_valid, xs, w_gate.astype(BF16), w_up.astype(BF16), w_down.astype(BF16))


def _router_kernel(a_ref, wo_ref, res_ref, g_ref, rt_ref,
                   x_ref, h_ref, idx_ref, gate_ref, rank_ref, cnt_ref, carry_ref):
    tm = res_ref.shape[0]
    ne = rt_ref.shape[0]

    @pl.when(pl.program_id(0) == 0)
    def _():
        carry_ref[...] = jnp.zeros_like(carry_ref)

    x = res_ref[...] + _dot(a_ref[...], wo_ref[...])
    x_ref[...] = x
    h = _rms(x, g_ref[...])
    _rows_to_slabs(h_ref, h)
    logits = _dot_nt(rt_ref[...], h.astype(BF16))
    eidx = lax.broadcasted_iota(jnp.int32, logits.shape, 0)
    m1 = jnp.max(logits, axis=0, keepdims=True)
    i1 = jnp.min(jnp.where(logits == m1, eidx, ne), axis=0, keepdims=True)
    rest = jnp.where(eidx == i1, -jnp.inf, logits)
    m2 = jnp.max(rest, axis=0, keepdims=True)
    i2 = jnp.min(jnp.where(rest == m2, eidx, ne), axis=0, keepdims=True)
    e2 = jnp.exp(m2 - m1)
    den = 1.0 + e2
    idx_ref[0] = jnp.concatenate([i1, i2], axis=0)
    gates = jnp.concatenate([1.0 / den, e2 / den, jnp.zeros((LANES - TOP_K, tm), F32)], axis=0)
    gate_ref[...] = gates.T[:, 0:TOP_K]

    sel1 = eidx == i1
    sel2 = eidx == i2
    member = jnp.where(sel1 | sel2, 1.0, 0.0)
    before = (lax.broadcasted_iota(jnp.int32, (tm, tm), 0) < lax.broadcasted_iota(jnp.int32, (tm, tm), 1))
    excl = _dot(member.astype(BF16), jnp.where(before, 1.0, 0.0).astype(BF16))
    excl = excl + carry_ref[:, 0:1]
    r1 = jnp.sum(jnp.where(sel1, excl, 0.0), axis=0, keepdims=True)
    r2 = jnp.sum(jnp.where(sel2, excl, 0.0), axis=0, keepdims=True)
    rank_ref[0] = jnp.concatenate([r1, r2], axis=0).astype(jnp.int32)
    total = carry_ref[:, 0:1] + jnp.sum(member, axis=1, keepdims=True)
    carry_ref[...] = jnp.broadcast_to(total, carry_ref.shape)
    cnt_ref[...] = jnp.broadcast_to(total, cnt_ref.shape).astype(jnp.int32)


def _outproj_router(a, w_out, res, gain, router, *, tm):
    n, d = res.shape
    k = a.shape[1]
    ne = router.shape[1]
    return pl.pallas_call(
        _router_kernel,
        grid=(n // tm,),
        in_specs=[
            pl.BlockSpec((tm, k), lambda i: (i, 0)),
            pl.BlockSpec((k, d), lambda i: (0, 0)),
            pl.BlockSpec((tm, d), lambda i: (i, 0)),
            pl.BlockSpec((1, d), lambda i: (0, 0)),
            pl.BlockSpec((ne, d), lambda i: (0, 0)),
        ],
        out_specs=[
            pl.BlockSpec((tm, d), lambda i: (i, 0)),
            pl.BlockSpec((tm * (d // LANES), LANES), lambda i: (i, 0)),
            pl.BlockSpec((1, TOP_K, tm), lambda i: (i, 0, 0)),
            pl.BlockSpec((tm, TOP_K), lambda i: (i, 0)),
            pl.BlockSpec((1, TOP_K, tm), lambda i: (i, 0, 0)),
            pl.BlockSpec((ne, 128), lambda i: (0, 0)),
        ],
        out_shape=[
            jax.ShapeDtypeStruct((n, d), F32),
            jax.ShapeDtypeStruct((n * (d // LANES), LANES), F32),
            jax.ShapeDtypeStruct((n // tm, TOP_K, tm), jnp.int32),
            jax.ShapeDtypeStruct((n, TOP_K), F32),
            jax.ShapeDtypeStruct((n // tm, TOP_K, tm), jnp.int32),
            jax.ShapeDtypeStruct((ne, 128), jnp.int32),
        ],
        scratch_shapes=[pltpu.VMEM((ne, 128), F32)],
        compiler_params=pltpu.CompilerParams(dimension_semantics=("arbitrary",)),
        name="router",
    )(a, w_out.astype(BF16), res, gain.reshape(1, d), router.T.astype(BF16))


def _scatter_kernel(pad_start_ref, pad_len_ref, pos_ref, h_ref, xs_ref, zero_ref, sem, zsem, *, tm_exp):
    tm = pos_ref.shape[2]
    dc = h_ref.shape[0] // tm
    ne = pad_start_ref.shape[0] - 1
    bits = [1 << b for b in reversed(range((tm_exp - 1).bit_length()))]

    def zero_copy(first_row, nrows):
        dst = pl.multiple_of(first_row * dc, dc)
        return pltpu.make_async_copy(zero_ref.at[pl.ds(0, nrows * dc)], xs_ref.at[pl.ds(dst, nrows * dc)], zsem)

    def for_each_zero_copy(act):
        for e in range(ne):
            for bit in bits:
                @pl.when((pad_len_ref[e] & bit) != 0)
                def _(e=e, bit=bit):
                    higher = pad_len_ref[e] & ~(2 * bit - 1)
                    act(zero_copy(pad_start_ref[e] + higher, bit))
        for t in range(ne):
            @pl.when(t < pad_len_ref[ne])
            def _(t=t):
                act(zero_copy(pad_start_ref[ne] + t * tm_exp, tm_exp))

    def row_copy(r, slot):
        src = pl.multiple_of(r * dc, dc)
        dst = pl.multiple_of(pos_ref[0, slot, r] * dc, dc)
        return pltpu.make_async_copy(h_ref.at[pl.ds(src, dc)], xs_ref.at[pl.ds(dst, dc)], sem)

    def start(r, carry):
        for slot in range(TOP_K):
            row_copy(r, slot).start(priority=slot)
        return carry

    def wait(r, carry):
        for slot in range(TOP_K):
            row_copy(r, slot).wait()
        return carry

    @pl.when(pl.program_id(0) == 0)
    def _():
        zero_ref[...] = jnp.zeros_like(zero_ref)
        for_each_zero_copy(lambda c: c.start())

    lax.fori_loop(0, tm, start, 0, unroll=8)

    @pl.when(pl.program_id(0) == 0)
    def _():
        for_each_zero_copy(lambda c: c.wait())

    lax.fori_loop(0, tm, wait, 0, unroll=8)


def _scatter_rows(h, pos_tiles, pad_start, pad_len, m_rows, *, tm, tm_exp):
    n = pos_tiles.shape[0] * tm
    dc = h.shape[0] // n
    grid_spec = pltpu.PrefetchScalarGridSpec(
        num_scalar_prefetch=2,
        grid=(n // tm,),
        in_specs=[
            pl.BlockSpec((1, TOP_K, tm), lambda i, ps, pn: (i, 0, 0), memory_space=pltpu.SMEM),
            pl.BlockSpec((tm * dc, LANES), lambda i, ps, pn: (i, 0)),
        ],
        out_specs=pl.BlockSpec(memory_space=pl.ANY),
        scratch_shapes=[pltpu.VMEM((tm_exp * dc, LANES), F32), pltpu.SemaphoreType.DMA(()),
                        pltpu.SemaphoreType.DMA(())],
    )
    return pl.pallas_call(
        functools.partial(_scatter_kernel, tm_exp=tm_exp),
        grid_spec=grid_spec,
        out_shape=jax.ShapeDtypeStruct((m_rows * dc, LANES), F32),
        compiler_params=pltpu.CompilerParams(dimension_semantics=("arbitrary",), has_side_effects=True),
        name="scatter_rows",
    )(pad_start, pad_len, pos_tiles, h)


def _combine_kernel(pos_ref, pos_next_ref, x_ref, gate_ref, g_ref, ys_ref, out_ref, rows_ref, sems):
    tm = x_ref.shape[0]
    i = pl.program_id(0)
    n_steps = pl.num_programs(0)
    buf = i % 2

    dc = rows_ref.shape[2] // tm

    def row_copy(p_ref, b, r, slot):
        src = pl.multiple_of(p_ref[0, slot, r] * dc, dc)
        dst = pl.multiple_of(r * dc, dc)
        return pltpu.make_async_copy(ys_ref.at[pl.ds(src, dc)], rows_ref.at[b, slot, pl.ds(dst, dc)], sems.at[b])

    def start_tile(p_ref, b):
        def body(r, carry):
            for slot in range(TOP_K):
                row_copy(p_ref, b, r, slot).start(priority=slot)
            return carry
        lax.fori_loop(0, tm, body, 0, unroll=8)

    @pl.when(i == 0)
    def _():
        start_tile(pos_ref, 0)

    @pl.when(i + 1 < n_steps)
    def _():
        start_tile(pos_next_ref, 1 - buf)

    def wait(r, carry):
        for slot in range(TOP_K):
            row_copy(pos_ref, buf, r, slot).wait()
        return carry

    lax.fori_loop(0, tm, wait, 0, unroll=8)
    y = (x_ref[...] + gate_ref[:, 0:1] * _slabs_to_rows(rows_ref.at[buf, 0], tm)
         + gate_ref[:, 1:2] * _slabs_to_rows(rows_ref.at[buf, 1], tm))
    out_ref[...] = _rms(y, g_ref[...])


def _combine(x2, pos_tiles, gate_col, gain, ys, *, tm):
    n, d = x2.shape
    dc = d // LANES
    nt = n // tm
    return pl.pallas_call(
        _combine_kernel,
        grid=(nt,),
        in_specs=[
            pl.BlockSpec((1, TOP_K, tm), lambda i: (i, 0, 0), memory_space=pltpu.SMEM),
            pl.BlockSpec((1, TOP_K, tm), lambda i: (jnp.minimum(i + 1, nt - 1), 0, 0), memory_space=pltpu.SMEM),
            pl.BlockSpec((tm, d), lambda i: (i, 0)),
            pl.BlockSpec((tm, TOP_K), lambda i: (i, 0)),
            pl.BlockSpec((1, d), lambda i: (0, 0)),
            pl.BlockSpec(memory_space=pl.ANY),
        ],
        out_specs=pl.BlockSpec((tm, d), lambda i: (i, 0)),
        out_shape=jax.ShapeDtypeStruct((n, d), F32),
        scratch_shapes=[pltpu.VMEM((2, TOP_K, tm * dc, LANES), F32), pltpu.SemaphoreType.DMA((2,))],
        compiler_params=pltpu.CompilerParams(dimension_semantics=("arbitrary",)),
        name="combine",
    )(pos_tiles, pos_tiles, x2, gate_col, gain.reshape(1, d), ys)


def _moe_and_final_norm(a, w_out, res, norm_gain, router, w_gate, w_up, w_down, final_gain,
                        *, tm_route, tm_exp, tf):
    n, d = res.shape
    ne = router.shape[1]
    x2, h, idx_t, gate_col, rank_t, counts = _outproj_router(a, w_out, res, norm_gain, router, tm=tm_route)

    cnt = counts[:, 0]
    padded = ((cnt + tm_exp - 1) // tm_exp) * tm_exp
    ends = jnp.cumsum(padded)
    starts = ends - padded
    pos_tiles = rank_t
    for e in range(ne):
        pos_tiles = pos_tiles + jnp.where(idx_t == e, starts[e], 0)
    m_rows = TOP_K * n + ne * tm_exp
    nt = m_rows // tm_exp
    n_valid = (ends[-1] // tm_exp).astype(jnp.int32)
    tile_blk = jnp.minimum(jnp.arange(nt, dtype=jnp.int32), n_valid - 1)
    tile_exp = jnp.minimum(
        jnp.sum(ends[None, :] <= (tile_blk * tm_exp)[:, None], axis=1).astype(jnp.int32), ne - 1)

    pad_start = jnp.concatenate([starts + cnt, ends[-1:]]).astype(jnp.int32)
    pad_len = jnp.concatenate([padded - cnt, (nt - n_valid).reshape(1)]).astype(jnp.int32)
    xs = _scatter_rows(h, pos_tiles, pad_start, pad_len, m_rows, tm=tm_route, tm_exp=tm_exp)
    ys = _experts(xs, tile_blk, tile_exp, n_valid.reshape(1), w_gate, w_up, w_down, tm=tm_exp, tf=tf)
    return _combine(x2, pos_tiles, gate_col, final_gain, ys, tm=tm_route)


def _trunk(x, l0_norm_mix, l0_ml_w_in, l0_ml_b_if, l0_ml_head_gain, l0_ml_w_out,
           l0_norm_ffn, l0_ffn_w_gate, l0_ffn_w_up, l0_ffn_w_down,
           l1_norm_mix, l1_gdn_w_in, l1_gdn_conv, l1_gdn_a_log, l1_gdn_dt_bias,
           l1_gdn_out_gain, l1_gdn_w_out, l1_norm_ffn, l1_moe_router,
           l1_moe_w_gate, l1_moe_w_up, l1_moe_w_down, final_norm, *, tiles):
    bsz, s, d = x.shape
    n = bsz * s
    nc = s // CHUNK
    x2 = x.reshape(n, d)

    qk_w = ML_HEADS * ML_DQK
    v_w = ML_HEADS * ML_DV
    fm = 2 * qk_w + 2 * v_w
    ng = 2 * ML_HEADS
    col_scale = jnp.concatenate([jnp.ones((qk_w,), F32), jnp.full((qk_w,), ML_DQK ** -0.5, F32),
                                 jnp.ones((2 * v_w + ng,), F32)])
    ne, _, f_exp = l1_moe_w_gate.shape
    pad_gates = lambda w: jnp.pad(w, ((0, 0), (0, LANES - ng))).astype(BF16)
    p, gcol, grow, moe_gate16, ffn_gate16, ffn_up16 = _inproj(
        x2, l0_norm_mix, pad_gates(l0_ml_w_in * col_scale), fm, ng, l0_ml_b_if, jnp.zeros((ng,), F32),
        _mlstm_gate_act, tm=tiles["tm_proj"], tn=tiles["tn_proj"],
        casts=(l1_moe_w_gate.reshape(ne * d, f_exp), l0_ffn_w_gate, l0_ffn_w_up))
    hs, moe_down16 = _mlstm(p.reshape(bsz, s, fm), gcol.reshape(bsz, s, ng), grow.reshape(bsz, nc, ng, CHUNK),
                            l0_ml_head_gain, bsz, s, l1_moe_w_down.reshape(ne * f_exp, d), bb=tiles["bb_ml"])

    x2 = _outproj_ffn(hs.reshape(n, v_w), l0_ml_w_out, x2, l0_norm_ffn, ffn_gate16, ffn_up16,
                      l0_ffn_w_down, tm=tiles["tm_ffn"], tf=tiles["tf"])

    kw = GD_HEADS * GD_DK
    vw = GD_HEADS * GD_DV
    fm = 2 * kw + 2 * vw
    ng = 2 * GD_HEADS
    zeros_h = jnp.zeros((GD_HEADS,), F32)
    pad_gates = lambda w: jnp.pad(w, ((0, 0), (0, LANES - ng))).astype(BF16)
    p, gcol, grow, moe_up16 = _inproj(
        x2, l1_norm_mix, pad_gates(l1_gdn_w_in), fm, ng, jnp.concatenate([zeros_h, l1_gdn_dt_bias]),
        jnp.concatenate([zeros_h, l1_gdn_a_log]), _gdn_gate_act, tm=tiles["tm_proj"], tn=tiles["tn_proj"],
        casts=(l1_moe_w_up.reshape(ne * d, f_exp),))
    o = _gdn(p.reshape(bsz, s, fm), gcol.reshape(bsz, s, ng), grow.reshape(bsz, nc, ng, CHUNK), l1_gdn_conv,
             l1_gdn_out_gain, bsz, s, bb=tiles["bb_gd"])

    out = _moe_and_final_norm(o.reshape(n, vw), l1_gdn_w_out, x2, l1_norm_ffn, l1_moe_router,
                              moe_gate16.reshape(ne, d, f_exp), moe_up16.reshape(ne, d, f_exp),
                              moe_down16.reshape(ne, f_exp, d), final_norm,
                              tm_route=tiles["tm_route"], tm_exp=tiles["tm_exp"], tf=tiles["tf_exp"])
    return out.reshape(bsz, s, d)


def _tiles_for(n, bsz, f):
    cap = lambda t: min(t, n)
    return dict(tm_proj=cap(512), tn_proj=1024, tm_ffn=cap(512), tf=f // 2,
                tm_route=cap(512), tm_exp=cap(256), tf_exp=f // 2, bb_ml=min(4, bsz), bb_gd=min(4, bsz))


def kernel(x, l0_norm_mix, l0_ml_w_in, l0_ml_b_if, l0_ml_head_gain, l0_ml_w_out, l0_norm_ffn, l0_ffn_w_gate, l0_ffn_w_up, l0_ffn_w_down, l1_norm_mix, l1_gdn_w_in, l1_gdn_conv, l1_gdn_a_log, l1_gdn_dt_bias, l1_gdn_out_gain, l1_gdn_w_out, l1_norm_ffn, l1_moe_router, l1_moe_w_gate, l1_moe_w_up, l1_moe_w_down, final_norm):
    n = x.shape[0] * x.shape[1]
    return _trunk(x, l0_norm_mix, l0_ml_w_in, l0_ml_b_if, l0_ml_head_gain, l0_ml_w_out,
                  l0_norm_ffn, l0_ffn_w_gate, l0_ffn_w_up, l0_ffn_w_down,
                  l1_norm_mix, l1_gdn_w_in, l1_gdn_conv, l1_gdn_a_log, l1_gdn_dt_bias,
                  l1_gdn_out_gain, l1_gdn_w_out, l1_norm_ffn, l1_moe_router,
                  l1_moe_w_gate, l1_moe_w_up, l1_moe_w_down, final_norm, tiles=_tiles_for(n, x.shape[0], l0_ffn_w_gate.shape[1]))
```

```python
import functools

import jax
import jax.numpy as jnp
from jax import lax
from jax.experimental import pallas as pl
from jax.experimental.pallas import tpu as pltpu

EPS = 1e-6
CHUNK = 64

ML_HEADS, ML_DQK, ML_DV = 4, 128, 256
GD_HEADS, GD_DK, GD_DV = 8, 128, 128
GD_CONV_K = 4
N_EXPERTS, TOP_K = 8, 2

F32 = jnp.float32
BF16 = jnp.bfloat16

_NT = (((1,), (1,)), ((), ()))
_TN = (((0,), (0,)), ((), ()))


def _dot(a, b):
    return jnp.dot(a, b, preferred_element_type=F32)


def _dot_nt(a, b):
    return lax.dot_general(a, b, _NT, preferred_element_type=F32)


def _dot_tn(a, b):
    return lax.dot_general(a, b, _TN, preferred_element_type=F32)


def _rms(x, g):
    return x * lax.rsqrt(jnp.mean(x * x, axis=-1, keepdims=True) + EPS) * g


def _softplus(x):
    return jnp.maximum(x, 0.0) + jnp.log(1.0 + jnp.exp(-jnp.abs(x)))


def _sigmoid(x):
    return 1.0 / (1.0 + jnp.exp(-x))


def _silu(x):
    return x * _sigmoid(x)


LANES = 128


def _rows_to_slabs(ref, x):
    rows, d = x.shape
    dc = d // LANES
    for c in range(dc):
        ref[pl.ds(c, rows, stride=dc), :] = x[:, c * LANES:(c + 1) * LANES]


def _slabs_to_rows(ref, rows):
    dc = ref.shape[0] // rows
    return jnp.concatenate([ref[pl.ds(c, rows, stride=dc), :] for c in range(dc)], axis=1)


def _mlstm_gate_act(pre, gidx, p1, p2):
    del p2
    z = pre + p1
    return jnp.where(gidx < ML_HEADS, z, -_softplus(-z))


def _gdn_gate_act(pre, gidx, p1, p2):
    beta = _sigmoid(pre)
    g = -(jnp.exp(p2) * _softplus(pre + p1))
    return jnp.where(gidx < GD_HEADS, beta, g)


def _cast_slices(src_refs, dst_refs):
    for src, dst in zip(src_refs, dst_refs):
        dst[...] = src[...].astype(dst.dtype)


def _cast_specs(arrays, steps, index_map):
    specs = [pl.BlockSpec((a.shape[0] // steps, a.shape[1]), index_map) for a in arrays]
    shapes = [jax.ShapeDtypeStruct(a.shape, BF16) for a in arrays]
    return specs, shapes


def _inproj_kernel(*refs, act, tn, n_cast):
    x_ref, g_ref, w_ref, pc_ref, pr_ref = refs[:5]
    cast_src = refs[5:5 + n_cast]
    main_ref, gcol_ref, grow_ref = refs[5 + n_cast:8 + n_cast]
    cast_dst = refs[8 + n_cast:]
    fm = main_ref.shape[1]
    ng = gcol_ref.shape[1]
    h = _rms(x_ref[...], g_ref[...]).astype(BF16)
    for c in range(0, fm, tn):
        main_ref[:, c:c + tn] = _dot(h, w_ref[:, c:c + tn]).astype(main_ref.dtype)
    pre = _dot(h, w_ref[:, fm:fm + LANES])
    gi_c = lax.broadcasted_iota(jnp.int32, pre.shape, 1)
    gcol_ref[...] = act(pre, gi_c, pc_ref[0:1, :], pc_ref[1:2, :])[:, 0:ng]
    pre_t = pre.T[0:ng, :]
    gi_r = lax.broadcasted_iota(jnp.int32, pre_t.shape, 0)
    rows = act(pre_t, gi_r, pr_ref[:, 0:1], pr_ref[:, 1:2])
    for c in range(grow_ref.shape[0]):
        grow_ref[c] = rows[:, c * CHUNK:(c + 1) * CHUNK]
    _cast_slices(cast_src, cast_dst)


def _inproj(x2, gain, w_all, fm, ng, p1, p2, act, *, tm, tn, casts=()):
    n, d = x2.shape
    pc = jnp.zeros((2, LANES), F32).at[:, :ng].set(jnp.stack([p1, p2]).astype(F32))
    pr = pc[:, :ng].T
    steps = n // tm
    cast_specs, cast_shapes = _cast_specs(casts, steps, lambda i: (i, 0))
    return pl.pallas_call(
        functools.partial(_inproj_kernel, act=act, tn=tn, n_cast=len(casts)),
        grid=(steps,),
        in_specs=[
            pl.BlockSpec((tm, d), lambda i: (i, 0)),
            pl.BlockSpec((1, d), lambda i: (0, 0)),
            pl.BlockSpec((d, fm + LANES), lambda i: (0, 0)),
            pl.BlockSpec((2, LANES), lambda i: (0, 0)),
            pl.BlockSpec((ng, 2), lambda i: (0, 0)),
        ] + cast_specs,
        out_specs=[
            pl.BlockSpec((tm, fm), lambda i: (i, 0)),
            pl.BlockSpec((tm, ng), lambda i: (i, 0)),
            pl.BlockSpec((tm // CHUNK, ng, CHUNK), lambda i: (i, 0, 0)),
        ] + cast_specs,
        out_shape=[
            jax.ShapeDtypeStruct((n, fm), BF16),
            jax.ShapeDtypeStruct((n, ng), F32),
            jax.ShapeDtypeStruct((n // CHUNK, ng, CHUNK), F32),
        ] + cast_shapes,
        compiler_params=pltpu.CompilerParams(dimension_semantics=("parallel",)),
        name="inproj",
    )(x2, gain.reshape(1, d), w_all, pc, pr, *casts)


def _cumsum_col_row(v_col, v_row, incl_lower, incl_upper):
    c_col = jnp.sum(jnp.where(incl_lower, v_row, 0.0), axis=1, keepdims=True)
    c_row = jnp.sum(jnp.where(incl_upper, v_col, 0.0), axis=0, keepdims=True)
    return c_col, c_row


def _mlstm_kernel(q_ref, k_ref, v_ref, o_ref, gcol_ref, grow_ref, gain_ref, cast_src, out_ref, cast_dst,
                  c_ref, n_ref, m_ref):
    L = CHUNK
    H = ML_HEADS
    _cast_slices([cast_src], [cast_dst])

    @pl.when(pl.program_id(1) == 0)
    def _():
        c_ref[...] = jnp.zeros_like(c_ref)
        n_ref[...] = jnp.zeros_like(n_ref)
        m_ref[...] = jnp.zeros_like(m_ref)

    row = lax.broadcasted_iota(jnp.int32, (L, L), 0)
    col = lax.broadcasted_iota(jnp.int32, (L, L), 1)
    lower = col <= row
    upper = row <= col
    units = [(bi, h) for bi in range(q_ref.shape[0]) for h in range(H)]
    hs = range(len(units))
    q = [q_ref[bi, :, h * ML_DQK:(h + 1) * ML_DQK] for bi, h in units]
    k = [k_ref[bi, :, h * ML_DQK:(h + 1) * ML_DQK] for bi, h in units]
    v = [v_ref[bi, :, h * ML_DV:(h + 1) * ML_DV] for bi, h in units]
    i_col = [gcol_ref[bi, :, h:h + 1] for bi, h in units]
    i_row = [grow_ref[bi, 0, h:h + 1, :] for bi, h in units]
    b = [_cumsum_col_row(gcol_ref[bi, :, H + h:H + h + 1], grow_ref[bi, 0, H + h:H + h + 1, :], lower, upper)
         for bi, h in units]
    b_col = [c for c, _ in b]
    b_row = [r for _, r in b]
    m_prev = [m_ref[h, 0:1, 0:1] for h in hs]
    c_prev = [c_ref[h] for h in hs]
    n_prev = [n_ref[h, 0:1, :] for h in hs]

    qk = [_dot_nt(q[h], k[h]) for h in hs]
    qc = [_dot(q[h], c_prev[h].astype(BF16)) for h in hs]
    d = [jnp.where(lower, b_col[h] - b_row[h] + i_row[h], -jnp.inf) for h in hs]
    m_inter = [b_col[h] + m_prev[h] for h in hs]
    m_t = [jnp.maximum(jnp.max(d[h], axis=1, keepdims=True), m_inter[h]) for h in hs]
    sc = [qk[h] * jnp.exp(d[h] - m_t[h]) for h in hs]
    a = [jnp.exp(m_inter[h] - m_t[h]) for h in hs]
    num = [_dot(sc[h].astype(BF16), v[h]) + a[h] * qc[h] for h in hs]
    den = [jnp.sum(sc[h], axis=1, keepdims=True)
           + a[h] * jnp.sum(q[h].astype(F32) * n_prev[h], axis=1, keepdims=True) for h in hs]
    hc = [num[h] / jnp.maximum(jnp.abs(den[h]), jnp.exp(-m_t[h])) for h in hs]

    b_last = [c[L - 1:L, :] for c in b_col]
    m_new = [jnp.maximum(b_last[h] + m_prev[h],
                         jnp.max(b_last[h] - b_row[h] + i_row[h], axis=1, keepdims=True)) for h in hs]
    ws_col = [jnp.exp(b_last[h] - b_col[h] + i_col[h] - m_new[h]) for h in hs]
    decay = [jnp.exp(b_last[h] + m_prev[h] - m_new[h]) for h in hs]
    wv = [(ws_col[h] * v[h].astype(F32)).astype(BF16) for h in hs]
    kv = [_dot_tn(k[h], wv[h]) for h in hs]
    for h in hs:
        c_ref[h] = decay[h] * c_prev[h] + kv[h]
        n_new = decay[h] * n_prev[h] + jnp.sum(ws_col[h] * k[h].astype(F32), axis=0, keepdims=True)
        n_ref[h] = jnp.broadcast_to(n_new, n_ref.shape[1:])
        m_ref[h] = jnp.broadcast_to(m_new[h], m_ref.shape[1:])
    for u, (bi, h) in enumerate(units):
        og = o_ref[bi, :, h * ML_DV:(h + 1) * ML_DV].astype(F32)
        gain = gain_ref[0:1, h * ML_DV:(h + 1) * ML_DV]
        y = _rms(hc[u], gain) * _sigmoid(og)
        out_ref[bi, :, h * ML_DV:(h + 1) * ML_DV] = y.astype(out_ref.dtype)


def _mlstm(p, gcol, grow, head_gain, bsz, s, cast, *, bb):
    nc = s // CHUNK
    qk_w = ML_HEADS * ML_DQK
    v_w = ML_HEADS * ML_DV
    ng = 2 * ML_HEADS
    (cast_spec,), (cast_shape,) = _cast_specs([cast], (bsz // bb) * nc, lambda b, c: (b * nc + c, 0))
    return pl.pallas_call(
        _mlstm_kernel,
        grid=(bsz // bb, nc),
        in_specs=[
            pl.BlockSpec((bb, CHUNK, qk_w), lambda b, c: (b, c, 0)),
            pl.BlockSpec((bb, CHUNK, qk_w), lambda b, c: (b, c, 1)),
            pl.BlockSpec((bb, CHUNK, v_w), lambda b, c: (b, c, 1)),
            pl.BlockSpec((bb, CHUNK, v_w), lambda b, c: (b, c, 2)),
            pl.BlockSpec((bb, CHUNK, ng), lambda b, c: (b, c, 0)),
            pl.BlockSpec((bb, 1, ng, CHUNK), lambda b, c: (b, c, 0, 0)),
            pl.BlockSpec((1, v_w), lambda b, c: (0, 0)),
            cast_spec,
        ],
        out_specs=[pl.BlockSpec((bb, CHUNK, v_w), lambda b, c: (b, c, 0)), cast_spec],
        out_shape=[jax.ShapeDtypeStruct((bsz, s, v_w), BF16), cast_shape],
        scratch_shapes=[
            pltpu.VMEM((bb * ML_HEADS, ML_DQK, ML_DV), F32),
            pltpu.VMEM((bb * ML_HEADS, 8, ML_DQK), F32),
            pltpu.VMEM((bb * ML_HEADS, 8, 128), F32),
        ],
        compiler_params=pltpu.CompilerParams(dimension_semantics=("parallel", "arbitrary")),
        name="mlstm",
    )(p, p, p, p, gcol, grow, head_gain.reshape(1, v_w).astype(F32), cast)


def _unit_lower_inverse_all(a_list):
    L = a_list[0].shape[0]
    eye = (lax.broadcasted_iota(jnp.int32, (L, L), 0) == lax.broadcasted_iota(jnp.int32, (L, L), 1)).astype(F32)
    xs = [-a for a in a_list]
    ps = [eye + x for x in xs]
    steps = max(1, (L - 1).bit_length()) - 1
    for _ in range(steps):
        x16 = [x.astype(BF16) for x in xs]
        xs = [_dot(xb, xb) for xb in x16]
        x16 = [x.astype(BF16) for x in xs]
        ps = [p + _dot(p.astype(BF16), xb) for p, xb in zip(ps, x16)]
    return ps


def _gdn_kernel(q_ref, k_ref, v_ref, z_ref, gcol_ref, grow_ref, cw_ref, gain_ref, out_ref,
                s_ref, halo_ref):
    L = CHUNK
    H = GD_HEADS
    kw = H * GD_DK

    @pl.when(pl.program_id(1) == 0)
    def _():
        s_ref[...] = jnp.zeros_like(s_ref)
        halo_ref[...] = jnp.zeros_like(halo_ref)

    halo_rows = halo_ref.shape[2]
    nsh = GD_CONV_K - 1
    rr = lax.broadcasted_iota(jnp.int32, (nsh * L, halo_rows + L), 0)
    cc = lax.broadcasted_iota(jnp.int32, (nsh * L, halo_rows + L), 1)
    shifts = jnp.where(cc == halo_rows + (rr % L) - (rr // L + 1), 1.0, 0.0).astype(BF16)

    def conv_silu(x_ref, part, bi):
        xb = x_ref[bi]
        width = xb.shape[1]
        ext = jnp.concatenate([halo_ref[bi, part, :, 0:width], xb], axis=0)
        r = _dot(shifts, ext)
        cw = cw_ref[:, part * kw:part * kw + width]
        acc = xb.astype(F32) * cw[GD_CONV_K - 1:GD_CONV_K, :]
        for sft in range(1, GD_CONV_K):
            acc = acc + r[(sft - 1) * L:sft * L, :] * cw[GD_CONV_K - 1 - sft:GD_CONV_K - sft, :]
        halo_ref[bi, part, :, 0:width] = xb[L - halo_rows:L, :]
        return _silu(acc)

    bb = q_ref.shape[0]
    qa = [conv_silu(q_ref, 0, bi) for bi in range(bb)]
    ka = [conv_silu(k_ref, 1, bi) for bi in range(bb)]
    va = [conv_silu(v_ref, 2, bi) for bi in range(bb)]

    row = lax.broadcasted_iota(jnp.int32, (L, L), 0)
    col = lax.broadcasted_iota(jnp.int32, (L, L), 1)
    lower = col <= row
    upper = row <= col
    strict = col < row
    units = [(bi, h) for bi in range(bb) for h in range(H)]
    hs = range(len(units))
    beta = [gcol_ref[bi, :, h:h + 1] for bi, h in units]
    gc = [_cumsum_col_row(gcol_ref[bi, :, H + h:H + h + 1], grow_ref[bi, 0, H + h:H + h + 1, :], lower, upper)
          for bi, h in units]
    gc_col = [c for c, _ in gc]
    gamma = [jnp.where(lower, jnp.exp(jnp.where(lower, c - r, 0.0)), 0.0) for c, r in gc]
    eg = [jnp.exp(c) for c in gc_col]
    g_last = [c[L - 1:L, :] for c in gc_col]

    q = [qa[bi][:, h * GD_DK:(h + 1) * GD_DK] for bi, h in units]
    k = [ka[bi][:, h * GD_DK:(h + 1) * GD_DK] for bi, h in units]
    v = [va[bi][:, h * GD_DV:(h + 1) * GD_DV] for bi, h in units]
    q = [x * lax.rsqrt(jnp.sum(x * x, axis=1, keepdims=True) + EPS) * (GD_DK ** -0.5) for x in q]
    k = [x * lax.rsqrt(jnp.sum(x * x, axis=1, keepdims=True) + EPS) for x in k]
    kb = [k[h] * beta[h] for h in hs]
    k16 = [x.astype(BF16) for x in k]

    kq = [_dot_nt(jnp.concatenate([kb[h], q[h]], axis=0).astype(BF16), k16[h]) for h in hs]
    a = [jnp.where(strict, kq[h][0:L] * gamma[h], 0.0) for h in hs]
    attn = [(kq[h][L:2 * L] * gamma[h]).astype(BF16) for h in hs]
    t = [x.astype(BF16) for x in _unit_lower_inverse_all(a)]
    uw = [_dot(t[h], jnp.concatenate([v[h] * beta[h], kb[h] * eg[h]], axis=1).astype(BF16)) for h in hs]

    s_prev = [s_ref[h] for h in hs]
    s16 = [x.astype(BF16) for x in s_prev]
    ws = [_dot(jnp.concatenate([uw[h][:, GD_DV:], q[h] * eg[h]], axis=0).astype(BF16), s16[h]) for h in hs]
    vn16 = [(uw[h][:, 0:GD_DV] - ws[h][0:L]).astype(BF16) for h in hs]
    o = [ws[h][L:2 * L] + _dot(attn[h], vn16[h]) for h in hs]
    kdec = [(k[h] * jnp.exp(g_last[h] - gc_col[h])).astype(BF16) for h in hs]
    for h in hs:
        s_ref[h] = jnp.exp(g_last[h]) * s_prev[h] + _dot_tn(kdec[h], vn16[h])
    for u, (bi, h) in enumerate(units):
        z = z_ref[bi, :, h * GD_DV:(h + 1) * GD_DV].astype(F32)
        y = _rms(o[u], gain_ref[...]) * _silu(z)
        out_ref[bi, :, h * GD_DV:(h + 1) * GD_DV] = y.astype(out_ref.dtype)


def _gdn(p, gcol, grow, conv_w, out_gain, bsz, s, *, bb):
    nc = s // CHUNK
    kw = GD_HEADS * GD_DK
    vw = GD_HEADS * GD_DV
    ng = 2 * GD_HEADS
    cc = conv_w.shape[1]
    return pl.pallas_call(
        _gdn_kernel,
        grid=(bsz // bb, nc),
        in_specs=[
            pl.BlockSpec((bb, CHUNK, kw), lambda b, c: (b, c, 0)),
            pl.BlockSpec((bb, CHUNK, kw), lambda b, c: (b, c, 1)),
            pl.BlockSpec((bb, CHUNK, vw), lambda b, c: (b, c, 2)),
            pl.BlockSpec((bb, CHUNK, vw), lambda b, c: (b, c, 3)),
            pl.BlockSpec((bb, CHUNK, ng), lambda b, c: (b, c, 0)),
            pl.BlockSpec((bb, 1, ng, CHUNK), lambda b, c: (b, c, 0, 0)),
            pl.BlockSpec((GD_CONV_K, cc), lambda b, c: (0, 0)),
            pl.BlockSpec((1, GD_DV), lambda b, c: (0, 0)),
        ],
        out_specs=pl.BlockSpec((bb, CHUNK, vw), lambda b, c: (b, c, 0)),
        out_shape=jax.ShapeDtypeStruct((bsz, s, vw), BF16),
        scratch_shapes=[
            pltpu.VMEM((bb * GD_HEADS, GD_DK, GD_DV), F32),
            pltpu.VMEM((bb, 3, 16, kw), BF16),
        ],
        compiler_params=pltpu.CompilerParams(dimension_semantics=("parallel", "arbitrary")),
        name="gdn",
    )(p, p, p, p, gcol, grow, conv_w.astype(F32), out_gain.reshape(1, GD_DV).astype(F32))


def _ffn_kernel(a_ref, wo_ref, res_ref, g_ref, wg_ref, wu_ref, wd_ref, out_ref, *, tf):
    x1 = res_ref[...] + _dot(a_ref[...], wo_ref[...])
    h = _rms(x1, g_ref[...]).astype(BF16)
    y = x1
    for c in range(0, wg_ref.shape[1], tf):
        a = _silu(_dot(h, wg_ref[:, c:c + tf])) * _dot(h, wu_ref[:, c:c + tf])
        y = y + _dot(a.astype(BF16), wd_ref[c:c + tf, :])
    out_ref[...] = y


def _outproj_ffn(a, w_out, res, gain, w_gate, w_up, w_down, *, tm, tf):
    n, d = res.shape
    k = a.shape[1]
    f = w_gate.shape[1]
    once = dict(pipeline_mode=pl.Buffered(1))
    return pl.pallas_call(
        functools.partial(_ffn_kernel, tf=tf),
        grid=(n // tm,),
        in_specs=[
            pl.BlockSpec((tm, k), lambda i: (i, 0)),
            pl.BlockSpec((k, d), lambda i: (0, 0), **once),
            pl.BlockSpec((tm, d), lambda i: (i, 0)),
            pl.BlockSpec((1, d), lambda i: (0, 0)),
            pl.BlockSpec((d, f), lambda i: (0, 0), **once),
            pl.BlockSpec((d, f), lambda i: (0, 0), **once),
            pl.BlockSpec((f, d), lambda i: (0, 0), **once),
        ],
        out_specs=pl.BlockSpec((tm, d), lambda i: (i, 0)),
        out_shape=jax.ShapeDtypeStruct((n, d), F32),
        compiler_params=pltpu.CompilerParams(dimension_semantics=("parallel",)),
        name="ffn",
    )(a, w_out.astype(BF16), res, gain.reshape(1, d), w_gate.astype(BF16), w_up.astype(BF16),
      w_down.astype(BF16))


def _expert_kernel(blk_ref, exp_ref, nvalid_ref, x_ref, wg_ref, wu_ref, wd_ref, out_ref, *, tm, tf):
    del blk_ref, exp_ref
    i = pl.program_id(0)
    f = wg_ref.shape[2]

    @pl.when(i < nvalid_ref[0])
    def _():
        h = _slabs_to_rows(x_ref, tm).astype(BF16)
        y = None
        for c in range(0, f, tf):
            a = _silu(_dot(h, wg_ref[0, :, c:c + tf])) * _dot(h, wu_ref[0, :, c:c + tf])
            part = _dot(a.astype(BF16), wd_ref[0, c:c + tf, :])
            y = part if y is None else y + part
        _rows_to_slabs(out_ref, y)

    @pl.when(i >= nvalid_ref[0])
    def _():
        out_ref[...] = jnp.zeros_like(out_ref)


def _experts(xs, tile_blk, tile_exp, n_valid, w_gate, w_up, w_down, *, tm, tf):
    d = w_gate.shape[1]
    dc = d // LANES
    m = xs.shape[0] // dc
    f = w_gate.shape[2]
    nt = m // tm
    grid_spec = pltpu.PrefetchScalarGridSpec(
        num_scalar_prefetch=3,
        grid=(nt,),
        in_specs=[
            pl.BlockSpec((tm * dc, LANES), lambda i, blk, ex, nv: (blk[i], 0)),
            pl.BlockSpec((1, d, f), lambda i, blk, ex, nv: (ex[i], 0, 0)),
            pl.BlockSpec((1, d, f), lambda i, blk, ex, nv: (ex[i], 0, 0)),
            pl.BlockSpec((1, f, d), lambda i, blk, ex, nv: (ex[i], 0, 0)),
        ],
        out_specs=pl.BlockSpec((tm * dc, LANES), lambda i, blk, ex, nv: (i, 0)),
    )
    return pl.pallas_call(
        functools.partial(_expert_kernel, tm=tm, tf=tf),
        grid_spec=grid_spec,
        out_shape=jax.ShapeDtypeStruct((m * dc, LANES), F32),
        compiler_params=pltpu.CompilerParams(dimension_semantics=("arbitrary",)),
        name="experts",
    )(tile_blk, tile_exp, n_valid, xs, w_gate.astype(BF16), w_up.astype(BF16), w_down.astype(BF16))


def _router_kernel(a_ref, wo_ref, res_ref, g_ref, rt_ref,
                   x_ref, h_ref, idx_ref, gate_ref, rank_ref, cnt_ref, carry_ref):
    tm = res_ref.shape[0]
    ne = rt_ref.shape[0]

    @pl.when(pl.program_id(0) == 0)
    def _():
        carry_ref[...] = jnp.zeros_like(carry_ref)

    x = res_ref[...] + _dot(a_ref[...], wo_ref[...])
    x_ref[...] = x
    h = _rms(x, g_ref[...])
    _rows_to_slabs(h_ref, h)
    logits = _dot_nt(rt_ref[...], h.astype(BF16))
    eidx = lax.broadcasted_iota(jnp.int32, logits.shape, 0)
    m1 = jnp.max(logits, axis=0, keepdims=True)
    i1 = jnp.min(jnp.where(logits == m1, eidx, ne), axis=0, keepdims=True)
    rest = jnp.where(eidx == i1, -jnp.inf, logits)
    m2 = jnp.max(rest, axis=0, keepdims=True)
    i2 = jnp.min(jnp.where(rest == m2, eidx, ne), axis=0, keepdims=True)
    e2 = jnp.exp(m2 - m1)
    den = 1.0 + e2
    idx_ref[0] = jnp.concatenate([i1, i2], axis=0)
    gates = jnp.concatenate([1.0 / den, e2 / den, jnp.zeros((LANES - TOP_K, tm), F32)], axis=0)
    gate_ref[...] = gates.T[:, 0:TOP_K]

    sel1 = eidx == i1
    sel2 = eidx == i2
    member = jnp.where(sel1 | sel2, 1.0, 0.0)
    before = (lax.broadcasted_iota(jnp.int32, (tm, tm), 0) < lax.broadcasted_iota(jnp.int32, (tm, tm), 1))
    excl = _dot(member.astype(BF16), jnp.where(before, 1.0, 0.0).astype(BF16))
    excl = excl + carry_ref[:, 0:1]
    r1 = jnp.sum(jnp.where(sel1, excl, 0.0), axis=0, keepdims=True)
    r2 = jnp.sum(jnp.where(sel2, excl, 0.0), axis=0, keepdims=True)
    rank_ref[0] = jnp.concatenate([r1, r2], axis=0).astype(jnp.int32)
    total = carry_ref[:, 0:1] + jnp.sum(member, axis=1, keepdims=True)
    carry_ref[...] = jnp.broadcast_to(total, carry_ref.shape)
    cnt_ref[...] = jnp.broadcast_to(total, cnt_ref.shape).astype(jnp.int32)


def _outproj_router(a, w_out, res, gain, router, *, tm):
    n, d = res.shape
    k = a.shape[1]
    ne = router.shape[1]
    return pl.pallas_call(
        _router_kernel,
        grid=(n // tm,),
        in_specs=[
            pl.BlockSpec((tm, k), lambda i: (i, 0)),
            pl.BlockSpec((k, d), lambda i: (0, 0)),
            pl.BlockSpec((tm, d), lambda i: (i, 0)),
            pl.BlockSpec((1, d), lambda i: (0, 0)),
            pl.BlockSpec((ne, d), lambda i: (0, 0)),
        ],
        out_specs=[
            pl.BlockSpec((tm, d), lambda i: (i, 0)),
            pl.BlockSpec((tm * (d // LANES), LANES), lambda i: (i, 0)),
            pl.BlockSpec((1, TOP_K, tm), lambda i: (i, 0, 0)),
            pl.BlockSpec((tm, TOP_K), lambda i: (i, 0)),
            pl.BlockSpec((1, TOP_K, tm), lambda i: (i, 0, 0)),
            pl.BlockSpec((ne, 128), lambda i: (0, 0)),
        ],
        out_shape=[
            jax.ShapeDtypeStruct((n, d), F32),
            jax.ShapeDtypeStruct((n * (d // LANES), LANES), F32),
            jax.ShapeDtypeStruct((n // tm, TOP_K, tm), jnp.int32),
            jax.ShapeDtypeStruct((n, TOP_K), F32),
            jax.ShapeDtypeStruct((n // tm, TOP_K, tm), jnp.int32),
            jax.ShapeDtypeStruct((ne, 128), jnp.int32),
        ],
        scratch_shapes=[pltpu.VMEM((ne, 128), F32)],
        compiler_params=pltpu.CompilerParams(dimension_semantics=("arbitrary",)),
        name="router",
    )(a, w_out.astype(BF16), res, gain.reshape(1, d), router.T.astype(BF16))


def _scatter_kernel(pad_start_ref, pad_len_ref, pos_ref, h_ref, xs_ref, zero_ref, sem, zsem, *, tm_exp):
    tm = pos_ref.shape[2]
    dc = h_ref.shape[0] // tm
    ne = pad_start_ref.shape[0] - 1
    bits = [1 << b for b in reversed(range((tm_exp - 1).bit_length()))]

    def zero_copy(first_row, nrows):
        dst = pl.multiple_of(first_row * dc, dc)
        return pltpu.make_async_copy(zero_ref.at[pl.ds(0, nrows * dc)], xs_ref.at[pl.ds(dst, nrows * dc)], zsem)

    def for_each_zero_copy(act):
        for e in range(ne):
            for bit in bits:
                @pl.when((pad_len_ref[e] & bit) != 0)
                def _(e=e, bit=bit):
                    higher = pad_len_ref[e] & ~(2 * bit - 1)
                    act(zero_copy(pad_start_ref[e] + higher, bit))
        for t in range(ne):
            @pl.when(t < pad_len_ref[ne])
            def _(t=t):
                act(zero_copy(pad_start_ref[ne] + t * tm_exp, tm_exp))

    def row_copy(r, slot):
        src = pl.multiple_of(r * dc, dc)
        dst = pl.multiple_of(pos_ref[0, slot, r] * dc, dc)
        return pltpu.make_async_copy(h_ref.at[pl.ds(src, dc)], xs_ref.at[pl.ds(dst, dc)], sem)

    def start(r, carry):
        for slot in range(TOP_K):
            row_copy(r, slot).start(priority=slot)
        return carry

    def wait(r, carry):
        for slot in range(TOP_K):
            row_copy(r, slot).wait()
        return carry

    @pl.when(pl.program_id(0) == 0)
    def _():
        zero_ref[...] = jnp.zeros_like(zero_ref)
        for_each_zero_copy(lambda c: c.start())

    lax.fori_loop(0, tm, start, 0, unroll=8)

    @pl.when(pl.program_id(0) == 0)
    def _():
        for_each_zero_copy(lambda c: c.wait())

    lax.fori_loop(0, tm, wait, 0, unroll=8)


def _scatter_rows(h, pos_tiles, pad_start, pad_len, m_rows, *, tm, tm_exp):
    n = pos_tiles.shape[0] * tm
    dc = h.shape[0] // n
    grid_spec = pltpu.PrefetchScalarGridSpec(
        num_scalar_prefetch=2,
        grid=(n // tm,),
        in_specs=[
            pl.BlockSpec((1, TOP_K, tm), lambda i, ps, pn: (i, 0, 0), memory_space=pltpu.SMEM),
            pl.BlockSpec((tm * dc, LANES), lambda i, ps, pn: (i, 0)),
        ],
        out_specs=pl.BlockSpec(memory_space=pl.ANY),
        scratch_shapes=[pltpu.VMEM((tm_exp * dc, LANES), F32), pltpu.SemaphoreType.DMA(()),
                        pltpu.SemaphoreType.DMA(())],
    )
    return pl.pallas_call(
        functools.partial(_scatter_kernel, tm_exp=tm_exp),
        grid_spec=grid_spec,
        out_shape=jax.ShapeDtypeStruct((m_rows * dc, LANES), F32),
        compiler_params=pltpu.CompilerParams(dimension_semantics=("arbitrary",), has_side_effects=True),
        name="scatter_rows",
    )(pad_start, pad_len, pos_tiles, h)


def _combine_kernel(pos_ref, pos_next_ref, x_ref, gate_ref, g_ref, ys_ref, out_ref, rows_ref, sems):
    tm = x_ref.shape[0]
    i = pl.program_id(0)
    n_steps = pl.num_programs(0)
    buf = i % 2

    dc = rows_ref.shape[2] // tm

    def row_copy(p_ref, b, r, slot):
        src = pl.multiple_of(p_ref[0, slot, r] * dc, dc)
        dst = pl.multiple_of(r * dc, dc)
        return pltpu.make_async_copy(ys_ref.at[pl.ds(src, dc)], rows_ref.at[b, slot, pl.ds(dst, dc)], sems.at[b])

    def start_tile(p_ref, b):
        def body(r, carry):
            for slot in range(TOP_K):
                row_copy(p_ref, b, r, slot).start(priority=slot)
            return carry
        lax.fori_loop(0, tm, body, 0, unroll=8)

    @pl.when(i == 0)
    def _():
        start_tile(pos_ref, 0)

    @pl.when(i + 1 < n_steps)
    def _():
        start_tile(pos_next_ref, 1 - buf)

    def wait(r, carry):
        for slot in range(TOP_K):
            row_copy(pos_ref, buf, r, slot).wait()
        return carry

    lax.fori_loop(0, tm, wait, 0, unroll=8)
    y = (x_ref[...] + gate_ref[:, 0:1] * _slabs_to_rows(rows_ref.at[buf, 0], tm)
         + gate_ref[:, 1:2] * _slabs_to_rows(rows_ref.at[buf, 1], tm))
    out_ref[...] = _rms(y, g_ref[...])


def _combine(x2, pos_tiles, gate_col, gain, ys, *, tm):
    n, d = x2.shape
    dc = d // LANES
    nt = n // tm
    return pl.pallas_call(
        _combine_kernel,
        grid=(nt,),
        in_specs=[
            pl.BlockSpec((1, TOP_K, tm), lambda i: (i, 0, 0), memory_space=pltpu.SMEM),
            pl.BlockSpec((1, TOP_K, tm), lambda i: (jnp.minimum(i + 1, nt - 1), 0, 0), memory_space=pltpu.SMEM),
            pl.BlockSpec((tm, d), lambda i: (i, 0)),
            pl.BlockSpec((tm, TOP_K), lambda i: (i, 0)),
            pl.BlockSpec((1, d), lambda i: (0, 0)),
            pl.BlockSpec(memory_space=pl.ANY),
        ],
        out_specs=pl.BlockSpec((tm, d), lambda i: (i, 0)),
        out_shape=jax.ShapeDtypeStruct((n, d), F32),
        scratch_shapes=[pltpu.VMEM((2, TOP_K, tm * dc, LANES), F32), pltpu.SemaphoreType.DMA((2,))],
        compiler_params=pltpu.CompilerParams(dimension_semantics=("arbitrary",)),
        name="combine",
    )(pos_tiles, pos_tiles, x2, gate_col, gain.reshape(1, d), ys)


def _moe_and_final_norm(a, w_out, res, norm_gain, router, w_gate, w_up, w_down, final_gain,
                        *, tm_route, tm_exp, tf):
    n, d = res.shape
    ne = router.shape[1]
    x2, h, idx_t, gate_col, rank_t, counts = _outproj_router(a, w_out, res, norm_gain, router, tm=tm_route)

    cnt = counts[:, 0]
    padded = ((cnt + tm_exp - 1) // tm_exp) * tm_exp
    ends = jnp.cumsum(padded)
    starts = ends - padded
    pos_tiles = rank_t
    for e in range(ne):
        pos_tiles = pos_tiles + jnp.where(idx_t == e, starts[e], 0)
    m_rows = TOP_K * n + ne * tm_exp
    nt = m_rows // tm_exp
    n_valid = (ends[-1] // tm_exp).astype(jnp.int32)
    tile_blk = jnp.minimum(jnp.arange(nt, dtype=jnp.int32), n_valid - 1)
    tile_exp = jnp.minimum(
        jnp.sum(ends[None, :] <= (tile_blk * tm_exp)[:, None], axis=1).astype(jnp.int32), ne - 1)

    pad_start = jnp.concatenate([starts + cnt, ends[-1:]]).astype(jnp.int32)
    pad_len = jnp.concatenate([padded - cnt, (nt - n_valid).reshape(1)]).astype(jnp.int32)
    xs = _scatter_rows(h, pos_tiles, pad_start, pad_len, m_rows, tm=tm_route, tm_exp=tm_exp)
    ys = _experts(xs, tile_blk, tile_exp, n_valid.reshape(1), w_gate, w_up, w_down, tm=tm_exp, tf=tf)
    return _combine(x2, pos_tiles, gate_col, final_gain, ys, tm=tm_route)


def _trunk(x, l0_norm_mix, l0_ml_w_in, l0_ml_b_if, l0_ml_head_gain, l0_ml_w_out,
           l0_norm_ffn, l0_ffn_w_gate, l0_ffn_w_up, l0_ffn_w_down,
           l1_norm_mix, l1_gdn_w_in, l1_gdn_conv, l1_gdn_a_log, l1_gdn_dt_bias,
           l1_gdn_out_gain, l1_gdn_w_out, l1_norm_ffn, l1_moe_router,
           l1_moe_w_gate, l1_moe_w_up, l1_moe_w_down, final_norm, *, tiles):
    bsz, s, d = x.shape
    n = bsz * s
    nc = s // CHUNK
    x2 = x.reshape(n, d)

    qk_w = ML_HEADS * ML_DQK
    v_w = ML_HEADS * ML_DV
    fm = 2 * qk_w + 2 * v_w
    ng = 2 * ML_HEADS
    col_scale = jnp.concatenate([jnp.ones((qk_w,), F32), jnp.full((qk_w,), ML_DQK ** -0.5, F32),
                                 jnp.ones((2 * v_w + ng,), F32)])
    ne, _, f_exp = l1_moe_w_gate.shape
    pad_gates = lambda w: jnp.pad(w, ((0, 0), (0, LANES - ng))).astype(BF16)
    p, gcol, grow, moe_gate16, ffn_gate16, ffn_up16 = _inproj(
        x2, l0_norm_mix, pad_gates(l0_ml_w_in * col_scale), fm, ng, l0_ml_b_if, jnp.zeros((ng,), F32),
        _mlstm_gate_act, tm=tiles["tm_proj"], tn=tiles["tn_proj"],
        casts=(l1_moe_w_gate.reshape(ne * d, f_exp), l0_ffn_w_gate, l0_ffn_w_up))
    hs, moe_down16 = _mlstm(p.reshape(bsz, s, fm), gcol.reshape(bsz, s, ng), grow.reshape(bsz, nc, ng, CHUNK),
                            l0_ml_head_gain, bsz, s, l1_moe_w_down.reshape(ne * f_exp, d), bb=tiles["bb_ml"])

    x2 = _outproj_ffn(hs.reshape(n, v_w), l0_ml_w_out, x2, l0_norm_ffn, ffn_gate16, ffn_up16,
                      l0_ffn_w_down, tm=tiles["tm_ffn"], tf=tiles["tf"])

    kw = GD_HEADS * GD_DK
    vw = GD_HEADS * GD_DV
    fm = 2 * kw + 2 * vw
    ng = 2 * GD_HEADS
    zeros_h = jnp.zeros((GD_HEADS,), F32)
    pad_gates = lambda w: jnp.pad(w, ((0, 0), (0, LANES - ng))).astype(BF16)
    p, gcol, grow, moe_up16 = _inproj(
        x2, l1_norm_mix, pad_gates(l1_gdn_w_in), fm, ng, jnp.concatenate([zeros_h, l1_gdn_dt_bias]),
        jnp.concatenate([zeros_h, l1_gdn_a_log]), _gdn_gate_act, tm=tiles["tm_proj"], tn=tiles["tn_proj"],
        casts=(l1_moe_w_up.reshape(ne * d, f_exp),))
    o = _gdn(p.reshape(bsz, s, fm), gcol.reshape(bsz, s, ng), grow.reshape(bsz, nc, ng, CHUNK), l1_gdn_conv,
             l1_gdn_out_gain, bsz, s, bb=tiles["bb_gd"])

    out = _moe_and_final_norm(o.reshape(n, vw), l1_gdn_w_out, x2, l1_norm_ffn, l1_moe_router,
                              moe_gate16.reshape(ne, d, f_exp), moe_up16.reshape(ne, d, f_exp),
                              moe_down16.reshape(ne, f_exp, d), final_norm,
                              tm_route=tiles["tm_route"], tm_exp=tiles["tm_exp"], tf=tiles["tf_exp"])
    return out.reshape(bsz, s, d)


def _tiles_for(n, bsz, f):
    cap = lambda t: min(t, n)
    return dict(tm_proj=cap(512), tn_proj=1024, tm_ffn=cap(512), tf=f // 2,
                tm_route=cap(512), tm_exp=cap(256), tf_exp=f // 2, bb_ml=min(8, bsz), bb_gd=min(8, bsz))


def kernel(x, l0_norm_mix, l0_ml_w_in, l0_ml_b_if, l0_ml_head_gain, l0_ml_w_out, l0_norm_ffn, l0_ffn_w_gate, l0_ffn_w_up, l0_ffn_w_down, l1_norm_mix, l1_gdn_w_in, l1_gdn_conv, l1_gdn_a_log, l1_gdn_dt_bias, l1_gdn_out_gain, l1_gdn_w_out, l1_norm_ffn, l1_moe_router, l1_moe_w_gate, l1_moe_w_up, l1_moe_w_down, final_norm):
    n = x.shape[0] * x.shape[1]
    return _trunk(x, l0_norm_mix, l0_ml_w_in, l0_ml_b_if, l0_ml_head_gain, l0_ml_w_out,
                  l0_norm_ffn, l0_ffn_w_gate, l0_ffn_w_up, l0_ffn_w_down,
                  l1_norm_mix, l1_gdn_w_in, l1_gdn_conv, l1_gdn_a_log, l1_gdn_dt_bias,
                  l1_gdn_out_gain, l1_gdn_w_out, l1_norm_ffn, l1_moe_router,
                  l1_moe_w_gate, l1_moe_w_up, l1_moe_w_down, final_norm, tiles=_tiles_for(n, x.shape[0], l0_ffn_w_gate.shape[1]))
```

```python
import functools

import jax
import jax.numpy as jnp
from jax import lax
from jax.experimental import pallas as pl
from jax.experimental.pallas import tpu as pltpu

EPS = 1e-6
CHUNK = 64

ML_HEADS, ML_DQK, ML_DV = 4, 128, 256
GD_HEADS, GD_DK, GD_DV = 8, 128, 128
GD_CONV_K = 4
N_EXPERTS, TOP_K = 8, 2

F32 = jnp.float32
BF16 = jnp.bfloat16

_NT = (((1,), (1,)), ((), ()))
_TN = (((0,), (0,)), ((), ()))


def _dot(a, b):
    return jnp.dot(a, b, preferred_element_type=F32)


def _dot_nt(a, b):
    return lax.dot_general(a, b, _NT, preferred_element_type=F32)


def _dot_tn(a, b):
    return lax.dot_general(a, b, _TN, preferred_element_type=F32)


def _rms(x, g):
    return x * lax.rsqrt(jnp.mean(x * x, axis=-1, keepdims=True) + EPS) * g


def _softplus(x):
    return jnp.maximum(x, 0.0) + jnp.log(1.0 + jnp.exp(-jnp.abs(x)))


def _sigmoid(x):
    return 1.0 / (1.0 + jnp.exp(-x))


def _silu(x):
    return x * _sigmoid(x)


LANES = 128


def _rows_to_slabs(ref, x):
    rows, d = x.shape
    dc = d // LANES
    for c in range(dc):
        ref[pl.ds(c, rows, stride=dc), :] = x[:, c * LANES:(c + 1) * LANES]


def _slabs_to_rows(ref, rows):
    dc = ref.shape[0] // rows
    return jnp.concatenate([ref[pl.ds(c, rows, stride=dc), :] for c in range(dc)], axis=1)


def _mlstm_gate_act(pre, gidx, p1, p2):
    del p2
    z = pre + p1
    return jnp.where(gidx < ML_HEADS, z, -_softplus(-z))


def _gdn_gate_act(pre, gidx, p1, p2):
    beta = _sigmoid(pre)
    g = -(jnp.exp(p2) * _softplus(pre + p1))
    return jnp.where(gidx < GD_HEADS, beta, g)


def _cast_slices(src_refs, dst_refs):
    for src, dst in zip(src_refs, dst_refs):
        dst[...] = src[...].astype(dst.dtype)


def _cast_specs(arrays, steps, index_map):
    specs = [pl.BlockSpec((a.shape[0] // steps, a.shape[1]), index_map) for a in arrays]
    shapes = [jax.ShapeDtypeStruct(a.shape, BF16) for a in arrays]
    return specs, shapes


def _inproj_kernel(*refs, act, tn, n_cast):
    x_ref, g_ref, w_ref, pc_ref, pr_ref = refs[:5]
    cast_src = refs[5:5 + n_cast]
    main_ref, gcol_ref, grow_ref = refs[5 + n_cast:8 + n_cast]
    cast_dst = refs[8 + n_cast:]
    fm = main_ref.shape[1]
    ng = gcol_ref.shape[1]
    h = _rms(x_ref[...], g_ref[...]).astype(BF16)
    for c in range(0, fm, tn):
        main_ref[:, c:c + tn] = _dot(h, w_ref[:, c:c + tn]).astype(main_ref.dtype)
    pre = _dot(h, w_ref[:, fm:fm + LANES])
    gi_c = lax.broadcasted_iota(jnp.int32, pre.shape, 1)
    gcol_ref[...] = act(pre, gi_c, pc_ref[0:1, :], pc_ref[1:2, :])[:, 0:ng]
    pre_t = pre.T[0:ng, :]
    gi_r = lax.broadcasted_iota(jnp.int32, pre_t.shape, 0)
    rows = act(pre_t, gi_r, pr_ref[:, 0:1], pr_ref[:, 1:2])
    for c in range(grow_ref.shape[0]):
        grow_ref[c] = rows[:, c * CHUNK:(c + 1) * CHUNK]
    _cast_slices(cast_src, cast_dst)


def _inproj(x2, gain, w_all, fm, ng, p1, p2, act, *, tm, tn, casts=()):
    n, d = x2.shape
    pc = jnp.zeros((2, LANES), F32).at[:, :ng].set(jnp.stack([p1, p2]).astype(F32))
    pr = pc[:, :ng].T
    steps = n // tm
    cast_specs, cast_shapes = _cast_specs(casts, steps, lambda i: (i, 0))
    return pl.pallas_call(
        functools.partial(_inproj_kernel, act=act, tn=tn, n_cast=len(casts)),
        grid=(steps,),
        in_specs=[
            pl.BlockSpec((tm, d), lambda i: (i, 0)),
            pl.BlockSpec((1, d), lambda i: (0, 0)),
            pl.BlockSpec((d, fm + LANES), lambda i: (0, 0)),
            pl.BlockSpec((2, LANES), lambda i: (0, 0)),
            pl.BlockSpec((ng, 2), lambda i: (0, 0)),
        ] + cast_specs,
        out_specs=[
            pl.BlockSpec((tm, fm), lambda i: (i, 0)),
            pl.BlockSpec((tm, ng), lambda i: (i, 0)),
            pl.BlockSpec((tm // CHUNK, ng, CHUNK), lambda i: (i, 0, 0)),
        ] + cast_specs,
        out_shape=[
            jax.ShapeDtypeStruct((n, fm), BF16),
            jax.ShapeDtypeStruct((n, ng), F32),
            jax.ShapeDtypeStruct((n // CHUNK, ng, CHUNK), F32),
        ] + cast_shapes,
        compiler_params=pltpu.CompilerParams(dimension_semantics=("parallel",)),
        name="inproj",
    )(x2, gain.reshape(1, d), w_all, pc, pr, *casts)


def _cumsum_col_row(v_col, v_row, incl_lower, incl_upper):
    c_col = jnp.sum(jnp.where(incl_lower, v_row, 0.0), axis=1, keepdims=True)
    c_row = jnp.sum(jnp.where(incl_upper, v_col, 0.0), axis=0, keepdims=True)
    return c_col, c_row


def _mlstm_kernel(q_ref, k_ref, v_ref, o_ref, gcol_ref, grow_ref, gain_ref, cast_src, out_ref, cast_dst,
                  c_ref, n_ref, m_ref):
    L = CHUNK
    H = ML_HEADS
    _cast_slices([cast_src], [cast_dst])

    @pl.when(pl.program_id(1) == 0)
    def _():
        c_ref[...] = jnp.zeros_like(c_ref)
        n_ref[...] = jnp.zeros_like(n_ref)
        m_ref[...] = jnp.zeros_like(m_ref)

    row = lax.broadcasted_iota(jnp.int32, (L, L), 0)
    col = lax.broadcasted_iota(jnp.int32, (L, L), 1)
    lower = col <= row
    upper = row <= col
    units = [(bi, h) for bi in range(q_ref.shape[0]) for h in range(H)]
    hs = range(len(units))
    q = [q_ref[bi, :, h * ML_DQK:(h + 1) * ML_DQK] for bi, h in units]
    k = [k_ref[bi, :, h * ML_DQK:(h + 1) * ML_DQK] for bi, h in units]
    v = [v_ref[bi, :, h * ML_DV:(h + 1) * ML_DV] for bi, h in units]
    i_col = [gcol_ref[bi, :, h:h + 1] for bi, h in units]
    i_row = [grow_ref[bi, 0, h:h + 1, :] for bi, h in units]
    b = [_cumsum_col_row(gcol_ref[bi, :, H + h:H + h + 1], grow_ref[bi, 0, H + h:H + h + 1, :], lower, upper)
         for bi, h in units]
    b_col = [c for c, _ in b]
    b_row = [r for _, r in b]
    m_prev = [m_ref[h, 0:1, 0:1] for h in hs]
    c_prev = [c_ref[h] for h in hs]
    n_prev = [n_ref[h, 0:1, :] for h in hs]

    qk = [_dot_nt(q[h], k[h]) for h in hs]
    qc = [_dot(q[h], c_prev[h].astype(BF16)) for h in hs]
    d = [jnp.where(lower, b_col[h] - b_row[h] + i_row[h], -jnp.inf) for h in hs]
    m_inter = [b_col[h] + m_prev[h] for h in hs]
    m_t = [jnp.maximum(jnp.max(d[h], axis=1, keepdims=True), m_inter[h]) for h in hs]
    sc = [qk[h] * jnp.exp(d[h] - m_t[h]) for h in hs]
    a = [jnp.exp(m_inter[h] - m_t[h]) for h in hs]
    num = [_dot(sc[h].astype(BF16), v[h]) + a[h] * qc[h] for h in hs]
    den = [jnp.sum(sc[h], axis=1, keepdims=True)
           + a[h] * jnp.sum(q[h].astype(F32) * n_prev[h], axis=1, keepdims=True) for h in hs]
    hc = [num[h] / jnp.maximum(jnp.abs(den[h]), jnp.exp(-m_t[h])) for h in hs]

    b_last = [c[L - 1:L, :] for c in b_col]
    m_new = [jnp.maximum(b_last[h] + m_prev[h],
                         jnp.max(b_last[h] - b_row[h] + i_row[h], axis=1, keepdims=True)) for h in hs]
    ws_col = [jnp.exp(b_last[h] - b_col[h] + i_col[h] - m_new[h]) for h in hs]
    decay = [jnp.exp(b_last[h] + m_prev[h] - m_new[h]) for h in hs]
    wv = [(ws_col[h] * v[h].astype(F32)).astype(BF16) for h in hs]
    kv = [_dot_tn(k[h], wv[h]) for h in hs]
    for h in hs:
        c_ref[h] = decay[h] * c_prev[h] + kv[h]
        n_new = decay[h] * n_prev[h] + jnp.sum(ws_col[h] * k[h].astype(F32), axis=0, keepdims=True)
        n_ref[h] = jnp.broadcast_to(n_new, n_ref.shape[1:])
        m_ref[h] = jnp.broadcast_to(m_new[h], m_ref.shape[1:])
    for u, (bi, h) in enumerate(units):
        og = o_ref[bi, :, h * ML_DV:(h + 1) * ML_DV].astype(F32)
        gain = gain_ref[0:1, h * ML_DV:(h + 1) * ML_DV]
        y = _rms(hc[u], gain) * _sigmoid(og)
        out_ref[bi, :, h * ML_DV:(h + 1) * ML_DV] = y.astype(out_ref.dtype)


def _mlstm(p, gcol, grow, head_gain, bsz, s, cast, *, bb):
    nc = s // CHUNK
    qk_w = ML_HEADS * ML_DQK
    v_w = ML_HEADS * ML_DV
    ng = 2 * ML_HEADS
    (cast_spec,), (cast_shape,) = _cast_specs([cast], (bsz // bb) * nc, lambda b, c: (b * nc + c, 0))
    return pl.pallas_call(
        _mlstm_kernel,
        grid=(bsz // bb, nc),
        in_specs=[
            pl.BlockSpec((bb, CHUNK, qk_w), lambda b, c: (b, c, 0)),
            pl.BlockSpec((bb, CHUNK, qk_w), lambda b, c: (b, c, 1)),
            pl.BlockSpec((bb, CHUNK, v_w), lambda b, c: (b, c, 1)),
            pl.BlockSpec((bb, CHUNK, v_w), lambda b, c: (b, c, 2)),
            pl.BlockSpec((bb, CHUNK, ng), lambda b, c: (b, c, 0)),
            pl.BlockSpec((bb, 1, ng, CHUNK), lambda b, c: (b, c, 0, 0)),
            pl.BlockSpec((1, v_w), lambda b, c: (0, 0)),
            cast_spec,
        ],
        out_specs=[pl.BlockSpec((bb, CHUNK, v_w), lambda b, c: (b, c, 0)), cast_spec],
        out_shape=[jax.ShapeDtypeStruct((bsz, s, v_w), BF16), cast_shape],
        scratch_shapes=[
            pltpu.VMEM((bb * ML_HEADS, ML_DQK, ML_DV), F32),
            pltpu.VMEM((bb * ML_HEADS, 8, ML_DQK), F32),
            pltpu.VMEM((bb * ML_HEADS, 8, 128), F32),
        ],
        compiler_params=pltpu.CompilerParams(dimension_semantics=("parallel", "arbitrary")),
        name="mlstm",
    )(p, p, p, p, gcol, grow, head_gain.reshape(1, v_w).astype(F32), cast)


def _unit_lower_inverse_all(a_list):
    L = a_list[0].shape[0]
    eye = (lax.broadcasted_iota(jnp.int32, (L, L), 0) == lax.broadcasted_iota(jnp.int32, (L, L), 1)).astype(F32)
    xs = [-a for a in a_list]
    ps = [eye + x for x in xs]
    steps = max(1, (L - 1).bit_length()) - 1
    for _ in range(steps):
        x16 = [x.astype(BF16) for x in xs]
        xs = [_dot(xb, xb) for xb in x16]
        x16 = [x.astype(BF16) for x in xs]
        ps = [p + _dot(p.astype(BF16), xb) for p, xb in zip(ps, x16)]
    return ps


def _gdn_kernel(q_ref, k_ref, v_ref, z_ref, gcol_ref, grow_ref, cw_ref, gain_ref, out_ref,
                s_ref, halo_ref):
    L = CHUNK
    H = GD_HEADS
    kw = H * GD_DK

    @pl.when(pl.program_id(1) == 0)
    def _():
        s_ref[...] = jnp.zeros_like(s_ref)
        halo_ref[...] = jnp.zeros_like(halo_ref)

    halo_rows = halo_ref.shape[2]
    nsh = GD_CONV_K - 1
    rr = lax.broadcasted_iota(jnp.int32, (nsh * L, halo_rows + L), 0)
    cc = lax.broadcasted_iota(jnp.int32, (nsh * L, halo_rows + L), 1)
    shifts = jnp.where(cc == halo_rows + (rr % L) - (rr // L + 1), 1.0, 0.0).astype(BF16)

    def conv_silu(x_ref, part, bi):
        xb = x_ref[bi]
        width = xb.shape[1]
        ext = jnp.concatenate([halo_ref[bi, part, :, 0:width], xb], axis=0)
        r = _dot(shifts, ext)
        cw = cw_ref[:, part * kw:part * kw + width]
        acc = xb.astype(F32) * cw[GD_CONV_K - 1:GD_CONV_K, :]
        for sft in range(1, GD_CONV_K):
            acc = acc + r[(sft - 1) * L:sft * L, :] * cw[GD_CONV_K - 1 - sft:GD_CONV_K - sft, :]
        halo_ref[bi, part, :, 0:width] = xb[L - halo_rows:L, :]
        return _silu(acc)

    bb = q_ref.shape[0]
    qa = [conv_silu(q_ref, 0, bi) for bi in range(bb)]
    ka = [conv_silu(k_ref, 1, bi) for bi in range(bb)]
    va = [conv_silu(v_ref, 2, bi) for bi in range(bb)]

    row = lax.broadcasted_iota(jnp.int32, (L, L), 0)
    col = lax.broadcasted_iota(jnp.int32, (L, L), 1)
    lower = col <= row
    upper = row <= col
    strict = col < row
    units = [(bi, h) for bi in range(bb) for h in range(H)]
    hs = range(len(units))
    beta = [gcol_ref[bi, :, h:h + 1] for bi, h in units]
    gc = [_cumsum_col_row(gcol_ref[bi, :, H + h:H + h + 1], grow_ref[bi, 0, H + h:H + h + 1, :], lower, upper)
          for bi, h in units]
    gc_col = [c for c, _ in gc]
    gamma = [jnp.where(lower, jnp.exp(jnp.where(lower, c - r, 0.0)), 0.0) for c, r in gc]
    eg = [jnp.exp(c) for c in gc_col]
    g_last = [c[L - 1:L, :] for c in gc_col]

    q = [qa[bi][:, h * GD_DK:(h + 1) * GD_DK] for bi, h in units]
    k = [ka[bi][:, h * GD_DK:(h + 1) * GD_DK] for bi, h in units]
    v = [va[bi][:, h * GD_DV:(h + 1) * GD_DV] for bi, h in units]
    q = [x * lax.rsqrt(jnp.sum(x * x, axis=1, keepdims=True) + EPS) * (GD_DK ** -0.5) for x in q]
    k = [x * lax.rsqrt(jnp.sum(x * x, axis=1, keepdims=True) + EPS) for x in k]
    kb = [k[h] * beta[h] for h in hs]
    k16 = [x.astype(BF16) for x in k]

    kq = [_dot_nt(jnp.concatenate([kb[h], q[h]], axis=0).astype(BF16), k16[h]) for h in hs]
    a = [jnp.where(strict, kq[h][0:L] * gamma[h], 0.0) for h in hs]
    attn = [(kq[h][L:2 * L] * gamma[h]).astype(BF16) for h in hs]
    t = [x.astype(BF16) for x in _unit_lower_inverse_all(a)]
    uw = [_dot(t[h], jnp.concatenate([v[h] * beta[h], kb[h] * eg[h]], axis=1).astype(BF16)) for h in hs]

    s_prev = [s_ref[h] for h in hs]
    s16 = [x.astype(BF16) for x in s_prev]
    ws = [_dot(jnp.concatenate([uw[h][:, GD_DV:], q[h] * eg[h]], axis=0).astype(BF16), s16[h]) for h in hs]
    vn16 = [(uw[h][:, 0:GD_DV] - ws[h][0:L]).astype(BF16) for h in hs]
    o = [ws[h][L:2 * L] + _dot(attn[h], vn16[h]) for h in hs]
    kdec = [(k[h] * jnp.exp(g_last[h] - gc_col[h])).astype(BF16) for h in hs]
    for h in hs:
        s_ref[h] = jnp.exp(g_last[h]) * s_prev[h] + _dot_tn(kdec[h], vn16[h])
    for u, (bi, h) in enumerate(units):
        z = z_ref[bi, :, h * GD_DV:(h + 1) * GD_DV].astype(F32)
        y = _rms(o[u], gain_ref[...]) * _silu(z)
        out_ref[bi, :, h * GD_DV:(h + 1) * GD_DV] = y.astype(out_ref.dtype)


def _gdn(p, gcol, grow, conv_w, out_gain, bsz, s, *, bb):
    nc = s // CHUNK
    kw = GD_HEADS * GD_DK
    vw = GD_HEADS * GD_DV
    ng = 2 * GD_HEADS
    cc = conv_w.shape[1]
    return pl.pallas_call(
        _gdn_kernel,
        grid=(bsz // bb, nc),
        in_specs=[
            pl.BlockSpec((bb, CHUNK, kw), lambda b, c: (b, c, 0)),
            pl.BlockSpec((bb, CHUNK, kw), lambda b, c: (b, c, 1)),
            pl.BlockSpec((bb, CHUNK, vw), lambda b, c: (b, c, 2)),
            pl.BlockSpec((bb, CHUNK, vw), lambda b, c: (b, c, 3)),
            pl.BlockSpec((bb, CHUNK, ng), lambda b, c: (b, c, 0)),
            pl.BlockSpec((bb, 1, ng, CHUNK), lambda b, c: (b, c, 0, 0)),
            pl.BlockSpec((GD_CONV_K, cc), lambda b, c: (0, 0)),
            pl.BlockSpec((1, GD_DV), lambda b, c: (0, 0)),
        ],
        out_specs=pl.BlockSpec((bb, CHUNK, vw), lambda b, c: (b, c, 0)),
        out_shape=jax.ShapeDtypeStruct((bsz, s, vw), BF16),
        scratch_shapes=[
            pltpu.VMEM((bb * GD_HEADS, GD_DK, GD_DV), F32),
            pltpu.VMEM((bb, 3, 16, kw), BF16),
        ],
        compiler_params=pltpu.CompilerParams(dimension_semantics=("parallel", "arbitrary")),
        name="gdn",
    )(p, p, p, p, gcol, grow, conv_w.astype(F32), out_gain.reshape(1, GD_DV).astype(F32))


def _ffn_kernel(a_ref, wo_ref, res_ref, g_ref, wg_ref, wu_ref, wd_ref, out_ref, *, tf):
    x1 = res_ref[...] + _dot(a_ref[...], wo_ref[...])
    h = _rms(x1, g_ref[...]).astype(BF16)
    y = x1
    for c in range(0, wg_ref.shape[1], tf):
        a = _silu(_dot(h, wg_ref[:, c:c + tf])) * _dot(h, wu_ref[:, c:c + tf])
        y = y + _dot(a.astype(BF16), wd_ref[c:c + tf, :])
    out_ref[...] = y


def _outproj_ffn(a, w_out, res, gain, w_gate, w_up, w_down, *, tm, tf):
    n, d = res.shape
    k = a.shape[1]
    f = w_gate.shape[1]
    once = dict(pipeline_mode=pl.Buffered(1))
    return pl.pallas_call(
        functools.partial(_ffn_kernel, tf=tf),
        grid=(n // tm,),
        in_specs=[
            pl.BlockSpec((tm, k), lambda i: (i, 0)),
            pl.BlockSpec((k, d), lambda i: (0, 0), **once),
            pl.BlockSpec((tm, d), lambda i: (i, 0)),
            pl.BlockSpec((1, d), lambda i: (0, 0)),
            pl.BlockSpec((d, f), lambda i: (0, 0), **once),
            pl.BlockSpec((d, f), lambda i: (0, 0), **once),
            pl.BlockSpec((f, d), lambda i: (0, 0), **once),
        ],
        out_specs=pl.BlockSpec((tm, d), lambda i: (i, 0)),
        out_shape=jax.ShapeDtypeStruct((n, d), F32),
        compiler_params=pltpu.CompilerParams(dimension_semantics=("parallel",)),
        name="ffn",
    )(a, w_out.astype(BF16), res, gain.reshape(1, d), w_gate.astype(BF16), w_up.astype(BF16),
      w_down.astype(BF16))


def _expert_kernel(blk_ref, exp_ref, nvalid_ref, x_ref, wg_ref, wu_ref, wd_ref, out_ref, *, tm, tf):
    del blk_ref, exp_ref
    i = pl.program_id(0)
    f = wg_ref.shape[2]

    @pl.when(i < nvalid_ref[0])
    def _():
        h = _slabs_to_rows(x_ref, tm).astype(BF16)
        y = None
        for c in range(0, f, tf):
            a = _silu(_dot(h, wg_ref[0, :, c:c + tf])) * _dot(h, wu_ref[0, :, c:c + tf])
            part = _dot(a.astype(BF16), wd_ref[0, c:c + tf, :])
            y = part if y is None else y + part
        _rows_to_slabs(out_ref, y)

    @pl.when(i >= nvalid_ref[0])
    def _():
        out_ref[...] = jnp.zeros_like(out_ref)


def _experts(xs, tile_blk, tile_exp, n_valid, w_gate, w_up, w_down, *, tm, tf):
    d = w_gate.shape[1]
    dc = d // LANES
    m = xs.shape[0] // dc
    f = w_gate.shape[2]
    nt = m // tm
    grid_spec = pltpu.PrefetchScalarGridSpec(
        num_scalar_prefetch=3,
        grid=(nt,),
        in_specs=[
            pl.BlockSpec((tm * dc, LANES), lambda i, blk, ex, nv: (blk[i], 0)),
            pl.BlockSpec((1, d, f), lambda i, blk, ex, nv: (ex[i], 0, 0)),
            pl.BlockSpec((1, d, f), lambda i, blk, ex, nv: (ex[i], 0, 0)),
            pl.BlockSpec((1, f, d), lambda i, blk, ex, nv: (ex[i], 0, 0)),
        ],
        out_specs=pl.BlockSpec((tm * dc, LANES), lambda i, blk, ex, nv: (i, 0)),
    )
    return pl.pallas_call(
        functools.partial(_expert_kernel, tm=tm, tf=tf),
        grid_spec=grid_spec,
        out_shape=jax.ShapeDtypeStruct((m * dc, LANES), F32),
        compiler_params=pltpu.CompilerParams(dimension_semantics=("arbitrary",)),
        name="experts",
    )(tile_blk, tile_exp, n_valid, xs, w_gate.astype(BF16), w_up.astype(BF16), w_down.astype(BF16))


def _router_kernel(a_ref, wo_ref, res_ref, g_ref, rt_ref,
                   x_ref, h_ref, idx_ref, gate_ref, rank_ref, cnt_ref, carry_ref):
    tm = res_ref.shape[0]
    ne = rt_ref.shape[0]

    @pl.when(pl.program_id(0) == 0)
    def _():
        carry_ref[...] = jnp.zeros_like(carry_ref)

    x = res_ref[...] + _dot(a_ref[...], wo_ref[...])
    x_ref[...] = x
    h = _rms(x, g_ref[...])
    _rows_to_slabs(h_ref, h)
    logits = _dot_nt(rt_ref[...], h.astype(BF16))
    eidx = lax.broadcasted_iota(jnp.int32, logits.shape, 0)
    m1 = jnp.max(logits, axis=0, keepdims=True)
    i1 = jnp.min(jnp.where(logits == m1, eidx, ne), axis=0, keepdims=True)
    rest = jnp.where(eidx == i1, -jnp.inf, logits)
    m2 = jnp.max(rest, axis=0, keepdims=True)
    i2 = jnp.min(jnp.where(rest == m2, eidx, ne), axis=0, keepdims=True)
    e2 = jnp.exp(m2 - m1)
    den = 1.0 + e2
    idx_ref[0] = jnp.concatenate([i1, i2], axis=0)
    gates = jnp.concatenate([1.0 / den, e2 / den, jnp.zeros((LANES - TOP_K, tm), F32)], axis=0)
    gate_ref[...] = gates.T[:, 0:TOP_K]

    sel1 = eidx == i1
    sel2 = eidx == i2
    member = jnp.where(sel1 | sel2, 1.0, 0.0)
    before = (lax.broadcasted_iota(jnp.int32, (tm, tm), 0) < lax.broadcasted_iota(jnp.int32, (tm, tm), 1))
    excl = _dot(member.astype(BF16), jnp.where(before, 1.0, 0.0).astype(BF16))
    excl = excl + carry_ref[:, 0:1]
    r1 = jnp.sum(jnp.where(sel1, excl, 0.0), axis=0, keepdims=True)
    r2 = jnp.sum(jnp.where(sel2, excl, 0.0), axis=0, keepdims=True)
    rank_ref[0] = jnp.concatenate([r1, r2], axis=0).astype(jnp.int32)
    total = carry_ref[:, 0:1] + jnp.sum(member, axis=1, keepdims=True)
    carry_ref[...] = jnp.broadcast_to(total, carry_ref.shape)
    cnt_ref[...] = jnp.broadcast_to(total, cnt_ref.shape).astype(jnp.int32)


def _outproj_router(a, w_out, res, gain, router, *, tm):
    n, d = res.shape
    k = a.shape[1]
    ne = router.shape[1]
    return pl.pallas_call(
        _router_kernel,
        grid=(n // tm,),
        in_specs=[
            pl.BlockSpec((tm, k), lambda i: (i, 0)),
            pl.BlockSpec((k, d), lambda i: (0, 0)),
            pl.BlockSpec((tm, d), lambda i: (i, 0)),
            pl.BlockSpec((1, d), lambda i: (0, 0)),
            pl.BlockSpec((ne, d), lambda i: (0, 0)),
        ],
        out_specs=[
            pl.BlockSpec((tm, d), lambda i: (i, 0)),
            pl.BlockSpec((tm * (d // LANES), LANES), lambda i: (i, 0)),
            pl.BlockSpec((1, TOP_K, tm), lambda i: (i, 0, 0)),
            pl.BlockSpec((tm, TOP_K), lambda i: (i, 0)),
            pl.BlockSpec((1, TOP_K, tm), lambda i: (i, 0, 0)),
            pl.BlockSpec((ne, 128), lambda i: (0, 0)),
        ],
        out_shape=[
            jax.ShapeDtypeStruct((n, d), F32),
            jax.ShapeDtypeStruct((n * (d // LANES), LANES), F32),
            jax.ShapeDtypeStruct((n // tm, TOP_K, tm), jnp.int32),
            jax.ShapeDtypeStruct((n, TOP_K), F32),
            jax.ShapeDtypeStruct((n // tm, TOP_K, tm), jnp.int32),
            jax.ShapeDtypeStruct((ne, 128), jnp.int32),
        ],
        scratch_shapes=[pltpu.VMEM((ne, 128), F32)],
        compiler_params=pltpu.CompilerParams(dimension_semantics=("arbitrary",)),
        name="router",
    )(a, w_out.astype(BF16), res, gain.reshape(1, d), router.T.astype(BF16))


def _scatter_kernel(pad_start_ref, pad_len_ref, pos_ref, h_ref, xs_ref, zero_ref, sem, zsem, *, tm_exp):
    tm = pos_ref.shape[2]
    dc = h_ref.shape[0] // tm
    ne = pad_start_ref.shape[0] - 1
    bits = [1 << b for b in reversed(range((tm_exp - 1).bit_length()))]

    def zero_copy(first_row, nrows):
        dst = pl.multiple_of(first_row * dc, dc)
        return pltpu.make_async_copy(zero_ref.at[pl.ds(0, nrows * dc)], xs_ref.at[pl.ds(dst, nrows * dc)], zsem)

    def for_each_zero_copy(act):
        for e in range(ne):
            for bit in bits:
                @pl.when((pad_len_ref[e] & bit) != 0)
                def _(e=e, bit=bit):
                    higher = pad_len_ref[e] & ~(2 * bit - 1)
                    act(zero_copy(pad_start_ref[e] + higher, bit))
        for t in range(ne):
            @pl.when(t < pad_len_ref[ne])
            def _(t=t):
                act(zero_copy(pad_start_ref[ne] + t * tm_exp, tm_exp))

    def row_copy(r, slot):
        src = pl.multiple_of(r * dc, dc)
        dst = pl.multiple_of(pos_ref[0, slot, r], dc)
        return pltpu.make_async_copy(h_ref.at[pl.ds(src, dc)], xs_ref.at[pl.ds(dst, dc)], sem)

    def start(r, carry):
        for slot in range(TOP_K):
            row_copy(r, slot).start(priority=slot)
        return carry

    def wait(r, carry):
        for slot in range(TOP_K):
            row_copy(r, slot).wait()
        return carry

    @pl.when(pl.program_id(0) == 0)
    def _():
        zero_ref[...] = jnp.zeros_like(zero_ref)
        for_each_zero_copy(lambda c: c.start())

    lax.fori_loop(0, tm, start, 0, unroll=8)

    @pl.when(pl.program_id(0) == 0)
    def _():
        for_each_zero_copy(lambda c: c.wait())

    lax.fori_loop(0, tm, wait, 0, unroll=8)


def _scatter_rows(h, pos_tiles, pad_start, pad_len, m_rows, *, tm, tm_exp):
    n = pos_tiles.shape[0] * tm
    dc = h.shape[0] // n
    grid_spec = pltpu.PrefetchScalarGridSpec(
        num_scalar_prefetch=2,
        grid=(n // tm,),
        in_specs=[
            pl.BlockSpec((1, TOP_K, tm), lambda i, ps, pn: (i, 0, 0), memory_space=pltpu.SMEM),
            pl.BlockSpec((tm * dc, LANES), lambda i, ps, pn: (i, 0)),
        ],
        out_specs=pl.BlockSpec(memory_space=pl.ANY),
        scratch_shapes=[pltpu.VMEM((tm_exp * dc, LANES), F32), pltpu.SemaphoreType.DMA(()),
                        pltpu.SemaphoreType.DMA(())],
    )
    return pl.pallas_call(
        functools.partial(_scatter_kernel, tm_exp=tm_exp),
        grid_spec=grid_spec,
        out_shape=jax.ShapeDtypeStruct((m_rows * dc, LANES), F32),
        compiler_params=pltpu.CompilerParams(dimension_semantics=("arbitrary",), has_side_effects=True),
        name="scatter_rows",
    )(pad_start, pad_len, pos_tiles, h)


def _combine_kernel(pos_ref, pos_next_ref, x_ref, gate_ref, g_ref, ys_ref, out_ref, rows_ref, sems):
    tm = x_ref.shape[0]
    i = pl.program_id(0)
    n_steps = pl.num_programs(0)
    buf = i % 2

    dc = rows_ref.shape[2] // tm

    def row_copy(p_ref, b, r, slot):
        src = pl.multiple_of(p_ref[0, slot, r], dc)
        dst = pl.multiple_of(r * dc, dc)
        return pltpu.make_async_copy(ys_ref.at[pl.ds(src, dc)], rows_ref.at[b, slot, pl.ds(dst, dc)], sems.at[b])

    def start_tile(p_ref, b):
        def body(r, carry):
            for slot in range(TOP_K):
                row_copy(p_ref, b, r, slot).start(priority=slot)
            return carry
        lax.fori_loop(0, tm, body, 0, unroll=8)

    @pl.when(i == 0)
    def _():
        start_tile(pos_ref, 0)

    @pl.when(i + 1 < n_steps)
    def _():
        start_tile(pos_next_ref, 1 - buf)

    def wait(r, carry):
        for slot in range(TOP_K):
            row_copy(pos_ref, buf, r, slot).wait()
        return carry

    lax.fori_loop(0, tm, wait, 0, unroll=8)
    y = (x_ref[...] + gate_ref[:, 0:1] * _slabs_to_rows(rows_ref.at[buf, 0], tm)
         + gate_ref[:, 1:2] * _slabs_to_rows(rows_ref.at[buf, 1], tm))
    out_ref[...] = _rms(y, g_ref[...])


def _combine(x2, pos_tiles, gate_col, gain, ys, *, tm):
    n, d = x2.shape
    dc = d // LANES
    nt = n // tm
    return pl.pallas_call(
        _combine_kernel,
        grid=(nt,),
        in_specs=[
            pl.BlockSpec((1, TOP_K, tm), lambda i: (i, 0, 0), memory_space=pltpu.SMEM),
            pl.BlockSpec((1, TOP_K, tm), lambda i: (jnp.minimum(i + 1, nt - 1), 0, 0), memory_space=pltpu.SMEM),
            pl.BlockSpec((tm, d), lambda i: (i, 0)),
            pl.BlockSpec((tm, TOP_K), lambda i: (i, 0)),
            pl.BlockSpec((1, d), lambda i: (0, 0)),
            pl.BlockSpec(memory_space=pl.ANY),
        ],
        out_specs=pl.BlockSpec((tm, d), lambda i: (i, 0)),
        out_shape=jax.ShapeDtypeStruct((n, d), F32),
        scratch_shapes=[pltpu.VMEM((2, TOP_K, tm * dc, LANES), F32), pltpu.SemaphoreType.DMA((2,))],
        compiler_params=pltpu.CompilerParams(dimension_semantics=("arbitrary",)),
        name="combine",
    )(pos_tiles, pos_tiles, x2, gate_col, gain.reshape(1, d), ys)


def _moe_and_final_norm(a, w_out, res, norm_gain, router, w_gate, w_up, w_down, final_gain,
                        *, tm_route, tm_exp, tf):
    n, d = res.shape
    ne = router.shape[1]
    x2, h, idx_t, gate_col, rank_t, counts = _outproj_router(a, w_out, res, norm_gain, router, tm=tm_route)

    cnt = counts[:, 0]
    padded = ((cnt + tm_exp - 1) // tm_exp) * tm_exp
    ends = jnp.cumsum(padded)
    starts = ends - padded
    pos_tiles = rank_t
    for e in range(ne):
        pos_tiles = pos_tiles + jnp.where(idx_t == e, starts[e], 0)
    m_rows = TOP_K * n + ne * tm_exp
    nt = m_rows // tm_exp
    n_valid = (ends[-1] // tm_exp).astype(jnp.int32)
    tile_blk = jnp.minimum(jnp.arange(nt, dtype=jnp.int32), n_valid - 1)
    tile_exp = jnp.minimum(
        jnp.sum(ends[None, :] <= (tile_blk * tm_exp)[:, None], axis=1).astype(jnp.int32), ne - 1)

    pad_start = jnp.concatenate([starts + cnt, ends[-1:]]).astype(jnp.int32)
    pad_len = jnp.concatenate([padded - cnt, (nt - n_valid).reshape(1)]).astype(jnp.int32)
    pos_tiles = pos_tiles * (d // LANES)
    xs = _scatter_rows(h, pos_tiles, pad_start, pad_len, m_rows, tm=tm_route, tm_exp=tm_exp)
    ys = _experts(xs, tile_blk, tile_exp, n_valid.reshape(1), w_gate, w_up, w_down, tm=tm_exp, tf=tf)
    return _combine(x2, pos_tiles, gate_col, final_gain, ys, tm=tm_route)


def _trunk(x, l0_norm_mix, l0_ml_w_in, l0_ml_b_if, l0_ml_head_gain, l0_ml_w_out,
           l0_norm_ffn, l0_ffn_w_gate, l0_ffn_w_up, l0_ffn_w_down,
           l1_norm_mix, l1_gdn_w_in, l1_gdn_conv, l1_gdn_a_log, l1_gdn_dt_bias,
           l1_gdn_out_gain, l1_gdn_w_out, l1_norm_ffn, l1_moe_router,
           l1_moe_w_gate, l1_moe_w_up, l1_moe_w_down, final_norm, *, tiles):
    bsz, s, d = x.shape
    n = bsz * s
    nc = s // CHUNK
    x2 = x.reshape(n, d)

    qk_w = ML_HEADS * ML_DQK
    v_w = ML_HEADS * ML_DV
    fm = 2 * qk_w + 2 * v_w
    ng = 2 * ML_HEADS
    col_scale = jnp.concatenate([jnp.ones((qk_w,), F32), jnp.full((qk_w,), ML_DQK ** -0.5, F32),
                                 jnp.ones((2 * v_w + ng,), F32)])
    ne, _, f_exp = l1_moe_w_gate.shape
    pad_gates = lambda w: jnp.pad(w, ((0, 0), (0, LANES - ng))).astype(BF16)
    p, gcol, grow, moe_gate16, ffn_gate16, ffn_up16 = _inproj(
        x2, l0_norm_mix, pad_gates(l0_ml_w_in * col_scale), fm, ng, l0_ml_b_if, jnp.zeros((ng,), F32),
        _mlstm_gate_act, tm=tiles["tm_proj"], tn=tiles["tn_proj"],
        casts=(l1_moe_w_gate.reshape(ne * d, f_exp), l0_ffn_w_gate, l0_ffn_w_up))
    hs, moe_down16 = _mlstm(p.reshape(bsz, s, fm), gcol.reshape(bsz, s, ng), grow.reshape(bsz, nc, ng, CHUNK),
                            l0_ml_head_gain, bsz, s, l1_moe_w_down.reshape(ne * f_exp, d), bb=tiles["bb_ml"])

    x2 = _outproj_ffn(hs.reshape(n, v_w), l0_ml_w_out, x2, l0_norm_ffn, ffn_gate16, ffn_up16,
                      l0_ffn_w_down, tm=tiles["tm_ffn"], tf=tiles["tf"])

    kw = GD_HEADS * GD_DK
    vw = GD_HEADS * GD_DV
    fm = 2 * kw + 2 * vw
    ng = 2 * GD_HEADS
    zeros_h = jnp.zeros((GD_HEADS,), F32)
    pad_gates = lambda w: jnp.pad(w, ((0, 0), (0, LANES - ng))).astype(BF16)
    p, gcol, grow, moe_up16 = _inproj(
        x2, l1_norm_mix, pad_gates(l1_gdn_w_in), fm, ng, jnp.concatenate([zeros_h, l1_gdn_dt_bias]),
        jnp.concatenate([zeros_h, l1_gdn_a_log]), _gdn_gate_act, tm=tiles["tm_proj"], tn=tiles["tn_proj"],
        casts=(l1_moe_w_up.reshape(ne * d, f_exp),))
    o = _gdn(p.reshape(bsz, s, fm), gcol.reshape(bsz, s, ng), grow.reshape(bsz, nc, ng, CHUNK), l1_gdn_conv,
             l1_gdn_out_gain, bsz, s, bb=tiles["bb_gd"])

    out = _moe_and_final_norm(o.reshape(n, vw), l1_gdn_w_out, x2, l1_norm_ffn, l1_moe_router,
                              moe_gate16.reshape(ne, d, f_exp), moe_up16.reshape(ne, d, f_exp),
                              moe_down16.reshape(ne, f_exp, d), final_norm,
                              tm_route=tiles["tm_route"], tm_exp=tiles["tm_exp"], tf=tiles["tf_exp"])
    return out.reshape(bsz, s, d)


def _tiles_for(n, bsz, f):
    cap = lambda t: min(t, n)
    return dict(tm_proj=cap(1024), tn_proj=1024, tm_ffn=cap(512), tf=f // 2,
                tm_route=cap(1024), tm_exp=cap(256), tf_exp=f // 2, bb_ml=min(8, bsz), bb_gd=min(8, bsz))


def kernel(x, l0_norm_mix, l0_ml_w_in, l0_ml_b_if, l0_ml_head_gain, l0_ml_w_out, l0_norm_ffn, l0_ffn_w_gate, l0_ffn_w_up, l0_ffn_w_down, l1_norm_mix, l1_gdn_w_in, l1_gdn_conv, l1_gdn_a_log, l1_gdn_dt_bias, l1_gdn_out_gain, l1_gdn_w_out, l1_norm_ffn, l1_moe_router, l1_moe_w_gate, l1_moe_w_up, l1_moe_w_down, final_norm):
    n = x.shape[0] * x.shape[1]
    return _trunk(x, l0_norm_mix, l0_ml_w_in, l0_ml_b_if, l0_ml_head_gain, l0_ml_w_out,
                  l0_norm_ffn, l0_ffn_w_gate, l0_ffn_w_up, l0_ffn_w_down,
                  l1_norm_mix, l1_gdn_w_in, l1_gdn_conv, l1_gdn_a_log, l1_gdn_dt_bias,
                  l1_gdn_out_gain, l1_gdn_w_out, l1_norm_ffn, l1_moe_router,
                  l1_moe_w_gate, l1_moe_w_up, l1_moe_w_down, final_norm, tiles=_tiles_for(n, x.shape[0], l0_ffn_w_gate.shape[1]))
```
